```python
import jax, jax.numpy as jnp
from jax import lax
import numpy as np

D_MODEL = 1024
BATCH = 4
SEQ = 4096
DEPTH = 4

ROPE_THETA = 500000.0
ROPE_DIM = 16
Q_BLOCK = 128
NORM_EPS = 1e-6

DSA_HEADS = 8
DSA_HEAD_DIM = 64
DSA_NOPE_DIM = DSA_HEAD_DIM - ROPE_DIM
DSA_KV_RANK = 128
DSA_V_DIM = 64
IDX_HEADS = 4
IDX_DIM = 64
INDEX_TOPK = 256

SB_HEADS = 8
SB_HEAD_DIM = 64

ML_HEADS = 4
ML_HEAD_DIM = 128
ML_CHUNK = 64
ML_CONV = 4

N_BRANCHES = 3
BRANCH_WIDTH = 512
DSA_Q_WIDTH = DSA_HEADS * DSA_HEAD_DIM
SB_WIDTH = SB_HEADS * SB_HEAD_DIM
ML_WIDTH = ML_HEADS * ML_HEAD_DIM

IN_WIDTHS = (
    DSA_Q_WIDTH,
    DSA_KV_RANK,
    ROPE_DIM,
    IDX_HEADS * IDX_DIM,
    IDX_DIM,
    IDX_HEADS,
    BRANCH_WIDTH,
    SB_WIDTH,
    SB_WIDTH,
    SB_WIDTH,
    BRANCH_WIDTH,
    2 * ML_WIDTH,
    ML_WIDTH,
    ML_HEADS,
    ML_HEADS,
    ML_WIDTH,
    BRANCH_WIDTH,
    N_BRANCHES * D_MODEL,
)
W_IN_COLS = sum(IN_WIDTHS)

kernel_name = 'hybrid_dsa_stickbreak_mlstm_trunk'


def rms_norm(x, w):
    x32 = x.astype(jnp.float32)
    y = x32 * lax.rsqrt(jnp.mean(x32 * x32, axis=-1, keepdims=True) + NORM_EPS)
    return (y * w.astype(jnp.float32)).astype(x.dtype)


def rope_tables(seq):
    pos = jnp.arange(seq, dtype=jnp.float32)
    inv = ROPE_THETA ** (-jnp.arange(0, ROPE_DIM, 2, dtype=jnp.float32) / ROPE_DIM)
    ang = pos[:, None] * inv[None, :]
    return jnp.cos(ang), jnp.sin(ang)


def partial_rope(x, cos, sin):
    half = ROPE_DIM // 2
    x1 = x[..., :half]
    x2 = x[..., half:ROPE_DIM]
    out = jnp.concatenate([x1 * cos - x2 * sin, x2 * cos + x1 * sin, x[..., ROPE_DIM:]], axis=-1)
    return out.astype(x.dtype)


def to_blocks(a):
    b, s = a.shape[:2]
    a = a.reshape((b, s // Q_BLOCK, Q_BLOCK) + a.shape[2:])
    return jnp.moveaxis(a, 1, 0)


def from_blocks(a):
    a = jnp.moveaxis(a, 0, 1)
    return a.reshape((a.shape[0], a.shape[1] * a.shape[2]) + a.shape[3:])


def dsa_attention(q, c_kv, k_rope, iq, ik, iw, w_uk, w_uv, kv_norm_w, cos, sin):
    bsz, seq = q.shape[:2]
    topk = min(INDEX_TOPK, seq // 4)
    c_kv = rms_norm(c_kv, kv_norm_w)
    q_rope = partial_rope(q[..., :ROPE_DIM], cos[:, None], sin[:, None])
    q_nope = q[..., ROPE_DIM:]
    k_rope = partial_rope(k_rope, cos, sin)
    iq = partial_rope(iq, cos[:, None], sin[:, None]).astype(jnp.float32)
    ik = partial_rope(ik, cos, sin).astype(jnp.float32)
    q_abs = jnp.einsum('bshn,rhn->bshr', q_nope, w_uk)
    q_cat = jnp.concatenate([q_abs, q_rope], axis=-1).astype(jnp.float32) * DSA_HEAD_DIM ** -0.5
    kv_cat = jnp.concatenate([c_kv, k_rope], axis=-1).astype(jnp.float32)
    key_pos = jnp.arange(seq)
    starts = jnp.arange(0, seq, Q_BLOCK)

    def block(args):
        qb, iqb, iwb, start = args
        q_pos = start + jnp.arange(Q_BLOCK)
        causal = key_pos[None, :] <= q_pos[:, None]
        score = jnp.einsum('bth,bths->bts', iwb,
                           jax.nn.relu(jnp.einsum('bthd,bsd->bths', iqb, ik)))
        score = jnp.where(causal[None], score, -jnp.inf)
        _, sel = lax.top_k(score, topk)
        kv_sel = jax.vmap(lambda kv, ix: kv[ix])(kv_cat, sel)
        valid = sel <= q_pos[None, :, None]
        logits = jnp.einsum('bthf,btkf->bthk', qb, kv_sel)
        logits = jnp.where(valid[:, :, None, :], logits, -jnp.inf)
        p = jax.nn.softmax(logits, axis=-1)
        return jnp.einsum('bthk,btkr->bthr', p, kv_sel[..., :DSA_KV_RANK])

    o_lat = from_blocks(lax.map(block, (to_blocks(q_cat), to_blocks(iq),
                                        to_blocks(iw.astype(jnp.float32)), starts)))
    o = jnp.einsum('bshr,rhv->bshv', o_lat, w_uv)
    return o.reshape(bsz, seq, DSA_HEADS * DSA_V_DIM)


def stick_breaking_attention(q, k, v):
    bsz, seq = q.shape[:2]
    qs = q.astype(jnp.float32) * SB_HEAD_DIM ** -0.5
    k32 = k.astype(jnp.float32)
    v32 = v.astype(jnp.float32)
    key_pos = jnp.arange(seq)
    starts = jnp.arange(0, seq, Q_BLOCK)

    def block(args):
        qb, start = args
        q_pos = start + jnp.arange(Q_BLOCK)
        strict = key_pos[None, :] < q_pos[:, None]
        z = jnp.einsum('bthd,bshd->bhts', qb, k32)
        log_stay = jnp.where(strict, jax.nn.log_sigmoid(-z), 0.0)
        log_after = lax.cumsum(log_stay, axis=3, reverse=True) - log_stay
        a = jnp.where(strict, jnp.exp(jax.nn.log_sigmoid(z) + log_after), 0.0)
        return jnp.einsum('bhts,bshd->bthd', a, v32)

    o = from_blocks(lax.map(block, (to_blocks(qs), starts)))
    return o.reshape(bsz, seq, SB_HEADS * SB_HEAD_DIM)


def causal_depthwise_conv(u, w):
    return lax.conv_general_dilated(u, w[:, None, :].astype(u.dtype), window_strides=(1,),
                                    padding=[(ML_CONV - 1, 0)],
                                    dimension_numbers=('NWC', 'WIO', 'NWC'),
                                    feature_group_count=u.shape[-1])


def mlstm_chunkwise(q, k, v, i_pre, f_pre):
    bsz, seq = q.shape[:2]
    nc = seq // ML_CHUNK

    def chunked(a):
        a = jnp.moveaxis(a.astype(jnp.float32), 2, 1)
        a = a.reshape((bsz, ML_HEADS, nc, ML_CHUNK) + a.shape[3:])
        return jnp.moveaxis(a, 2, 0)

    qc = chunked(q)
    kc = chunked(k.astype(jnp.float32) * ML_HEAD_DIM ** -0.5)
    vc = chunked(v)
    ic = chunked(i_pre)
    lfc = chunked(jax.nn.log_sigmoid(f_pre.astype(jnp.float32)))
    tril = jnp.tril(jnp.ones((ML_CHUNK, ML_CHUNK), dtype=bool))

    def step(carry, inp):
        c_mem, n_mem, m_prev = carry
        qb, kb, vb, ib, lfb = inp
        b = jnp.cumsum(lfb, axis=-1)
        log_d = jnp.where(tril, b[..., :, None] - b[..., None, :] + ib[..., None, :], -jnp.inf)
        log_inter = b + m_prev[..., None]
        m_row = jnp.maximum(jnp.max(log_d, axis=-1), log_inter)
        w_intra = jnp.exp(log_d - m_row[..., None])
        w_inter = jnp.exp(log_inter - m_row)
        s = jnp.einsum('bhtd,bhsd->bhts', qb, kb) * w_intra
        num = (jnp.einsum('bhts,bhse->bhte', s, vb)
               + w_inter[..., None] * jnp.einsum('bhtd,bhde->bhte', qb, c_mem))
        den = jnp.sum(s, axis=-1) + w_inter * jnp.einsum('bhtd,bhd->bht', qb, n_mem)
        h = num / jnp.maximum(jnp.abs(den), jnp.exp(-m_row))[..., None]
        b_last = b[..., -1]
        log_w = b_last[..., None] - b + ib
        m_new = jnp.maximum(b_last + m_prev, jnp.max(log_w, axis=-1))
        w_upd = jnp.exp(log_w - m_new[..., None])
        decay = jnp.exp(b_last + m_prev - m_new)
        c_mem = decay[..., None, None] * c_mem + jnp.einsum('bhs,bhsd,bhse->bhde', w_upd, kb, vb)
        n_mem = decay[..., None] * n_mem + jnp.einsum('bhs,bhsd->bhd', w_upd, kb)
        return (c_mem, n_mem, m_new), h

    init = (jnp.zeros((bsz, ML_HEADS, ML_HEAD_DIM, ML_HEAD_DIM), jnp.float32),
            jnp.zeros((bsz, ML_HEADS, ML_HEAD_DIM), jnp.float32),
            jnp.zeros((bsz, ML_HEADS), jnp.float32))
    _, h = lax.scan(step, init, (qc, kc, vc, ic, lfc))
    h = jnp.moveaxis(h, 0, 2).reshape(bsz, ML_HEADS, seq, ML_HEAD_DIM)
    return jnp.moveaxis(h, 1, 2)


def head_layer_norm(h, w):
    h = h.astype(jnp.float32)
    mu = jnp.mean(h, axis=-1, keepdims=True)
    var = jnp.mean(jnp.square(h - mu), axis=-1, keepdims=True)
    y = (h - mu) * lax.rsqrt(var + NORM_EPS)
    return y.reshape(h.shape[0], h.shape[1], -1) * w.astype(jnp.float32)


def hybrid_layer(x, w_in, w_uk, w_uv, kv_norm_w, conv_w, i_bias, f_bias, ml_norm_w,
                 w_branch, w_out, norm_w, cos, sin):
    bsz, seq, _ = x.shape
    h = rms_norm(x, norm_w)
    p = h @ w_in
    splits = [int(c) for c in np.cumsum(IN_WIDTHS)[:-1]]
    (dsa_q, dsa_ckv, dsa_krope, idx_q, idx_k, idx_w, dsa_z,
     sb_q, sb_k, sb_v, sb_z,
     ml_qk, ml_v, ml_i, ml_f, ml_o, ml_z, merge) = jnp.split(p, splits, axis=-1)

    def heads(a, n):
        return a.reshape(bsz, seq, n, -1)

    y_a = dsa_attention(heads(dsa_q, DSA_HEADS), dsa_ckv, dsa_krope, heads(idx_q, IDX_HEADS),
                        idx_k, idx_w, w_uk, w_uv, kv_norm_w, cos, sin)
    y_b = stick_breaking_attention(heads(sb_q, SB_HEADS), heads(sb_k, SB_HEADS), heads(sb_v, SB_HEADS))
    ml_q, ml_k = jnp.split(jax.nn.silu(causal_depthwise_conv(ml_qk, conv_w)), 2, axis=-1)
    h_c = mlstm_chunkwise(heads(ml_q, ML_HEADS), heads(ml_k, ML_HEADS), heads(ml_v, ML_HEADS),
                          ml_i + i_bias, ml_f + f_bias)
    y_c = head_layer_norm(h_c, ml_norm_w) * jax.nn.sigmoid(ml_o.astype(jnp.float32))

    branches = jnp.stack([y_a * jax.nn.silu(dsa_z), y_b * jax.nn.silu(sb_z),
                          y_c * jax.nn.silu(ml_z)], axis=2)
    proj = jnp.einsum('bsgc,gcd->bsgd', branches, w_branch)
    gates = jax.nn.sigmoid(merge.reshape(bsz, seq, N_BRANCHES, D_MODEL).astype(jnp.float32))
    mixed = jnp.sum(gates * proj, axis=2)
    return x + (mixed @ w_out).astype(x.dtype)


def setup_inputs(seed: int = 0) -> dict:
    key = jax.random.key(seed)
    ks = jax.random.split(key, 13)

    def nrm(k, shape, fan_in):
        return jax.random.normal(k, shape, jnp.float32) * fan_in ** -0.5

    x = jax.random.normal(ks[0], (BATCH, SEQ, D_MODEL), jnp.float32)
    w_in = nrm(ks[1], (DEPTH, D_MODEL, W_IN_COLS), D_MODEL)
    w_dsa_uk = nrm(ks[2], (DEPTH, DSA_KV_RANK, DSA_HEADS, DSA_NOPE_DIM), DSA_KV_RANK)
    w_dsa_uv = nrm(ks[3], (DEPTH, DSA_KV_RANK, DSA_HEADS, DSA_V_DIM), DSA_KV_RANK)
    dsa_kv_norm_w = 1.0 + 0.02 * jax.random.normal(ks[4], (DEPTH, DSA_KV_RANK), jnp.float32)
    ml_conv_w = nrm(ks[5], (DEPTH, ML_CONV, 2 * ML_WIDTH), ML_CONV)
    ml_i_bias = 0.1 * jax.random.normal(ks[6], (DEPTH, ML_HEADS), jnp.float32)
    ml_f_bias = (jnp.linspace(3.0, 6.0, ML_HEADS, dtype=jnp.float32)[None, :]
                 + 0.1 * jax.random.normal(ks[7], (DEPTH, ML_HEADS), jnp.float32))
    ml_norm_w = 1.0 + 0.02 * jax.random.normal(ks[8], (DEPTH, ML_WIDTH), jnp.float32)
    w_branch = nrm(ks[9], (DEPTH, N_BRANCHES, BRANCH_WIDTH, D_MODEL), BRANCH_WIDTH)
    w_out = nrm(ks[10], (DEPTH, D_MODEL, D_MODEL), D_MODEL)
    norm_w = 1.0 + 0.02 * jax.random.normal(ks[11], (DEPTH, D_MODEL), jnp.float32)
    final_norm_w = 1.0 + 0.02 * jax.random.normal(ks[12], (D_MODEL,), jnp.float32)
    return {'x': x, 'w_in': w_in, 'w_dsa_uk': w_dsa_uk, 'w_dsa_uv': w_dsa_uv,
            'dsa_kv_norm_w': dsa_kv_norm_w, 'ml_conv_w': ml_conv_w, 'ml_i_bias': ml_i_bias,
            'ml_f_bias': ml_f_bias, 'ml_norm_w': ml_norm_w, 'w_branch': w_branch,
            'w_out': w_out, 'norm_w': norm_w, 'final_norm_w': final_norm_w}


def reference(x, w_in, w_dsa_uk, w_dsa_uv, dsa_kv_norm_w, ml_conv_w, ml_i_bias, ml_f_bias,
              ml_norm_w, w_branch, w_out, norm_w, final_norm_w):
    cos, sin = rope_tables(x.shape[1])
    for l in range(DEPTH):
        x = hybrid_layer(x, w_in[l], w_dsa_uk[l], w_dsa_uv[l], dsa_kv_norm_w[l], ml_conv_w[l],
                         ml_i_bias[l], ml_f_bias[l], ml_norm_w[l], w_branch[l], w_out[l],
                         norm_w[l], cos, sin)
    return rms_norm(x, final_norm_w)
```

```python
import functools

import jax
import jax.numpy as jnp
from jax import lax
from jax.experimental import pallas as pl
from jax.experimental.pallas import tpu as pltpu

F32 = jnp.float32
BF16 = jnp.bfloat16
I32 = jnp.int32

D_MODEL = 1024
ROPE_THETA = 500000.0
ROPE_DIM = 16
NORM_EPS = 1e-6
DSA_HEADS = 8
DSA_HEAD_DIM = 64
DSA_NOPE_DIM = DSA_HEAD_DIM - ROPE_DIM
DSA_KV_RANK = 128
DSA_V_DIM = 64
IDX_HEADS = 4
IDX_DIM = 64
INDEX_TOPK = 256
SB_HEADS = 8
SB_HEAD_DIM = 64
ML_HEADS = 4
ML_HEAD_DIM = 128
ML_CONV = 4
N_BRANCHES = 3
BRANCH_WIDTH = 512

LANES = 128
VMEM_LIMIT = 56 * 1024 * 1024

_O_DSA_Q, _O_CKV, _O_KROPE, _O_IDX_Q, _O_IDX_K, _O_IDX_W = 0, 512, 640, 656, 912, 976
_O_DSA_Z, _O_SB_Q, _O_SB_K, _O_SB_V, _O_SB_Z = 980, 1492, 2004, 2516, 3028
_O_ML_QK, _O_ML_V, _O_ML_I, _O_ML_F, _O_ML_O, _O_ML_Z, _O_MERGE, _O_END = (
    3540, 4564, 5076, 5080, 5084, 5596, 6108, 9180)

C_DSA_Q, C_CKV, C_SMALL, C_IDX_Q, C_DSA_Z = 0, 512, 640, 768, 1024
C_SB_Q, C_SB_K, C_SB_V, C_SB_Z = 1536, 2048, 2560, 3072
C_ML_Q, C_ML_K, C_ML_V, C_ML_O, C_ML_Z, C_MERGE, P_COLS = 3584, 4096, 4608, 5120, 5632, 6144, 9216
SM_IK, SM_KR, SM_IW, SM_MI, SM_MF = 0, 64, 80, 84, 88

KEY_NEG_INF = -2139095041
INT_MIN = -2147483648
MASK_NEG = -1e30


def _dot(a, b):
    return jnp.dot(a, b, preferred_element_type=F32)


def _dot_nt(a, b):
    return lax.dot_general(a, b, (((1,), (1,)), ((), ())), preferred_element_type=F32)


def _split2(x):
    hi = x.astype(BF16)
    lo = (x - hi.astype(F32)).astype(BF16)
    return hi, lo


def _split3(x):
    hi = x.astype(BF16)
    r = x - hi.astype(F32)
    mid = r.astype(BF16)
    lo = (r - mid.astype(F32)).astype(BF16)
    return hi, mid, lo


def _log_sigmoid(x):
    return jnp.minimum(x, 0.0) - jnp.log(1.0 + jnp.exp(-jnp.abs(x)))


def _rope(x, cos, sin_a, sin_b):
    w = x.shape[1]
    reps = w // LANES
    if reps > 1:
        cos = jnp.concatenate([cos] * reps, axis=1)
        sin_a = jnp.concatenate([sin_a] * reps, axis=1)
        sin_b = jnp.concatenate([sin_b] * reps, axis=1)
    half = ROPE_DIM // 2
    return x * cos + pltpu.roll(x, w - half, 1) * sin_a + pltpu.roll(x, half, 1) * sin_b


def _in_proj_kernel(x_ref, nw_ref, w_ref, o_ref, h_ref):
    @pl.when(pl.program_id(1) == 0)
    def _():
        x = x_ref[...]
        ms = jnp.mean(x * x, axis=-1, keepdims=True)
        h_ref[...] = (x * lax.rsqrt(ms + NORM_EPS) * nw_ref[...]).astype(BF16)

    o_ref[...] = _dot(h_ref[...], w_ref[...])


def _in_proj(x2d, norm_w, w_bf16):
    m = x2d.shape[0]
    tm = min(1024, m)
    tn = 1024
    return pl.pallas_call(
        _in_proj_kernel,
        grid=(m // tm, P_COLS // tn),
        in_specs=[
            pl.BlockSpec((tm, D_MODEL), lambda i, j: (i, 0)),
            pl.BlockSpec((1, D_MODEL), lambda i, j: (0, 0)),
            pl.BlockSpec((D_MODEL, tn), lambda i, j: (0, j)),
        ],
        out_specs=pl.BlockSpec((tm, tn), lambda i, j: (i, j)),
        out_shape=jax.ShapeDtypeStruct((m, P_COLS), F32),
        scratch_shapes=[pltpu.VMEM((tm, D_MODEL), BF16)],
        compiler_params=pltpu.CompilerParams(
            dimension_semantics=("arbitrary", "arbitrary"), vmem_limit_bytes=VMEM_LIMIT),
        name="in_proj",
    )(x2d, norm_w.reshape(1, D_MODEL), w_bf16)


def _sortable_key(s):
    s = jnp.where(s == 0.0, 0.0, s)
    bits = lax.bitcast_convert_type(s, I32)
    return bits ^ ((bits >> 31) & 0x7FFFFFFF)


def _dsa_kernel(q_ref, ckv_ref, sm_ref, iq_ref, z_ref, cos_ref, sa_ref, sb_ref, kvw_ref, wq_ref,
                wuv_ref, out_ref, kv_s, ikh_s, ikl_s, sc_s, bias_s, qs_s, thr_s, js_s,
                m_s, l_s, acc_s, *, T, topk, seq):
    i = pl.program_id(1)
    nh = DSA_HEADS
    cos, sin_a, sin_b = cos_ref[...], sa_ref[...], sb_ref[...]
    lane = lax.broadcasted_iota(I32, (T, LANES), 1)

    sm = _rope(sm_ref[...], cos, sin_a, sin_b)
    ckv = ckv_ref[...]
    ckv_n = ckv * lax.rsqrt(jnp.mean(ckv * ckv, axis=-1, keepdims=True) + NORM_EPS) * kvw_ref[...]
    row0 = pl.multiple_of(i * T, T)
    kv_s[pl.ds(row0, T), 0:LANES] = ckv_n.astype(BF16)
    kr = jnp.where((lane >= SM_KR) & (lane < SM_KR + ROPE_DIM), sm, 0.0)
    kv_s[pl.ds(row0, T), LANES:2 * LANES] = kr.astype(BF16)
    ik_hi, ik_lo = _split2(jnp.where(lane < IDX_DIM, sm, 0.0))
    ikh_s[pl.ds(row0, T), :] = ik_hi
    ikl_s[pl.ds(row0, T), :] = ik_lo

    iq = _rope(iq_ref[...], cos, sin_a, sin_b)
    iq_parts = []
    for h in range(IDX_HEADS):
        t = iq[:, (h // 2) * LANES:(h // 2 + 1) * LANES]
        if h % 2:
            t = pltpu.roll(t, IDX_DIM, 1)
        iq_parts.append(_split2(t))
    w_cols = [sm[:, SM_IW + h:SM_IW + h + 1] for h in range(IDX_HEADS)]

    def score_tile(kt):
        r0 = pl.multiple_of(kt * T, T)
        kh = ikh_s[pl.ds(r0, T), :]
        kl = ikl_s[pl.ds(r0, T), :]
        s = jnp.zeros((T, T), F32)
        for h in range(IDX_HEADS):
            hi, lo = iq_parts[h]
            d = _dot_nt(hi, kh) + (_dot_nt(hi, kl) + _dot_nt(lo, kh))
            s = s + w_cols[h] * jnp.maximum(d, 0.0)
        return _sortable_key(s)

    def score_body(kt, carry):
        sc_s[kt] = score_tile(kt)
        return carry

    lax.fori_loop(0, i, score_body, 0)
    rr = lax.broadcasted_iota(I32, (T, T), 0)
    cc = lax.broadcasted_iota(I32, (T, T), 1)
    sc_s[i] = jnp.where(cc <= rr, score_tile(i), KEY_NEG_INF)

    thr_s[...] = jnp.full((T, 1), KEY_NEG_INF + 1, I32)
    js_s[...] = jnp.full((T, 1), seq, I32)
    RH = LANES

    @pl.when((i + 1) * T > topk)
    def _select():
        for hh in range(T // RH):
            r_lo = hh * RH

            def count(pred):
                def body(kt, acc):
                    tile = sc_s[kt, r_lo:r_lo + RH, :]
                    m = jnp.where(pred(tile, kt), 1, 0).astype(I32)
                    for c in range(T // LANES):
                        acc = acc + m[:, c * LANES:(c + 1) * LANES]
                    return acc
                acc = lax.fori_loop(0, i + 1, body, jnp.zeros((RH, LANES), I32))
                return jnp.sum(acc, axis=1, keepdims=True)

            def bit_body(it, ans):
                cand = ans + lax.shift_left(jnp.int32(1), 31 - it)
                c = count(lambda tile, kt: tile >= cand)
                return jnp.where(c >= topk, cand, ans)

            ans = lax.fori_loop(0, 32, bit_body, jnp.full((RH, 1), INT_MIN, I32))
            n_gt = count(lambda tile, kt: tile > ans)
            n_ge = count(lambda tile, kt: tile >= ans)
            need = topk - n_gt
            thr_s[r_lo:r_lo + RH, :] = jnp.maximum(ans, KEY_NEG_INF + 1)

            @pl.when(jnp.max(n_ge) > topk)
            def _ties():
                col = lax.broadcasted_iota(I32, (RH, T), 1)

                def pos_body(it, pos):
                    cand = pos + lax.shift_left(jnp.int32(1), (seq.bit_length() - 2) - it)
                    c = count(lambda tile, kt: (tile == ans) & (col + kt * T < cand))
                    return jnp.where(c < need, cand, pos)

                pos = lax.fori_loop(0, seq.bit_length() - 1, pos_body, jnp.zeros((RH, 1), I32))
                js_s[r_lo:r_lo + RH, :] = jnp.where(ans == KEY_NEG_INF, seq, pos)

    thr = thr_s[...]
    js = js_s[...]

    def bias_body(kt, carry):
        key = sc_s[kt]
        sel = (key > thr) | ((key == thr) & (cc + kt * T <= js))
        bias_s[kt] = jnp.where(sel, 0.0, MASK_NEG)
        return carry

    lax.fori_loop(0, i + 1, bias_body, 0)

    qr = _rope(q_ref[...], cos, sin_a, sin_b).astype(BF16)
    qcat = _dot(qr, wq_ref[...]).astype(BF16)
    for h in range(nh):
        qs_s[h * T:(h + 1) * T, :] = qcat[:, h * 2 * LANES:(h + 1) * 2 * LANES]
    m_s[...] = jnp.full((nh * T, 1), MASK_NEG, F32)
    l_s[...] = jnp.zeros((nh * T, 1), F32)
    acc_s[...] = jnp.zeros((nh * T, LANES), F32)

    def att_body(kt, carry):
        r0 = pl.multiple_of(kt * T, T)
        kv = kv_s[pl.ds(r0, T), :]
        lg = _dot_nt(qs_s[...], kv)
        lg = (lg.reshape(nh, T, T) + bias_s[kt][None]).reshape(nh * T, T)
        m_old = m_s[...]
        m_new = jnp.maximum(m_old, jnp.max(lg, axis=1, keepdims=True))
        alpha = jnp.exp(m_old - m_new)
        p = jnp.exp(lg - m_new)
        l_s[...] = alpha * l_s[...] + jnp.sum(p, axis=1, keepdims=True)
        acc_s[...] = alpha * acc_s[...] + _dot(p.astype(BF16), kv[:, 0:LANES])
        m_s[...] = m_new
        return carry

    lax.fori_loop(0, i + 1, att_body, 0)

    o = (acc_s[...] / l_s[...]).astype(BF16)
    ocat = jnp.concatenate([o[h * T:(h + 1) * T, :] for h in range(nh)], axis=1)
    y = _dot(ocat, wuv_ref[...])
    z = z_ref[...]
    out_ref[...] = (y * (z * jax.nn.sigmoid(z))).astype(BF16)


def _dsa(p3, cos, sin_a, sin_b, kv_norm_w, wq, wuv):
    b, s, _ = p3.shape
    T = min(256, s)
    topk = min(INDEX_TOPK, s // 4)
    nkt = s // T
    kern = functools.partial(_dsa_kernel, T=T, topk=topk, seq=s)

    def pspec(width, col):
        return pl.BlockSpec((None, T, width), lambda bb, i: (bb, i, col // width))

    def tspec():
        return pl.BlockSpec((T, LANES), lambda bb, i: (i, 0))

    return pl.pallas_call(
        kern,
        grid=(b, nkt),
        in_specs=[
            pspec(512, C_DSA_Q), pspec(LANES, C_CKV), pspec(LANES, C_SMALL), pspec(256, C_IDX_Q),
            pspec(512, C_DSA_Z), tspec(), tspec(), tspec(),
            pl.BlockSpec((1, DSA_KV_RANK), lambda bb, i: (0, 0)),
            pl.BlockSpec(wq.shape, lambda bb, i: (0, 0)),
            pl.BlockSpec(wuv.shape, lambda bb, i: (0, 0)),
        ],
        out_specs=pl.BlockSpec((None, T, BRANCH_WIDTH), lambda bb, i: (bb, i, 0)),
        out_shape=jax.ShapeDtypeStruct((b, s, BRANCH_WIDTH), BF16),
        scratch_shapes=[
            pltpu.VMEM((s, 2 * LANES), BF16),
            pltpu.VMEM((s, LANES), BF16),
            pltpu.VMEM((s, LANES), BF16),
            pltpu.VMEM((nkt, T, T), I32),
            pltpu.VMEM((nkt, T, T), F32),
            pltpu.VMEM((DSA_HEADS * T, 2 * LANES), BF16),
            pltpu.VMEM((T, 1), I32),
            pltpu.VMEM((T, 1), I32),
            pltpu.VMEM((DSA_HEADS * T, 1), F32),
            pltpu.VMEM((DSA_HEADS * T, 1), F32),
            pltpu.VMEM((DSA_HEADS * T, LANES), F32),
        ],
        compiler_params=pltpu.CompilerParams(
            dimension_semantics=("arbitrary", "arbitrary"), vmem_limit_bytes=VMEM_LIMIT),
        name="dsa",
    )(p3, p3, p3, p3, p3, cos, sin_a, sin_b, kv_norm_w.reshape(1, DSA_KV_RANK), wq, wuv)


def _sb_kernel(q_ref, k_ref, v_ref, z_ref, u_ref, out_ref, *, T):
    i = pl.program_id(2)
    lane = lax.broadcasted_iota(I32, (T, LANES), 1)
    rr = lax.broadcasted_iota(I32, (T, T), 0)
    cc = lax.broadcasted_iota(I32, (T, T), 1)
    strict = cc < rr
    q = q_ref[...]
    u = u_ref[...]
    heads_per_block = LANES // SB_HEAD_DIM
    result = jnp.zeros((T, LANES), F32)
    for hh in range(heads_per_block):
        in_head = (lane >= hh * SB_HEAD_DIM) & (lane < (hh + 1) * SB_HEAD_DIM)
        qh = jnp.where(in_head, q, 0.0).astype(BF16)

        def tile(kt, carry, acc, diag):
            r0 = pl.multiple_of(kt * T, T)
            k = k_ref[pl.ds(r0, T), :].astype(BF16)
            v = v_ref[pl.ds(r0, T), :].astype(BF16)
            zz = _dot_nt(qh, k)
            lsz = _log_sigmoid(zz)
            ls = lsz - zz
            if diag:
                ls = jnp.where(strict, ls, 0.0)
            hi, lo = _split2(ls)
            la = _dot(hi, u) + _dot(lo, u)
            a = jnp.exp(lsz + la + carry)
            if diag:
                a = jnp.where(strict, a, 0.0)
            acc = acc + _dot(a.astype(BF16), v)
            carry = carry + (la[:, 0:1] + ls[:, 0:1])
            return carry, acc

        carry, acc = tile(i, jnp.zeros((T, 1), F32), jnp.zeros((T, LANES), F32), True)

        def body(j, ca):
            return tile(i - 1 - j, ca[0], ca[1], False)

        carry, acc = lax.fori_loop(0, i, body, (carry, acc))
        result = jnp.where(in_head, acc, result)
    z = z_ref[...]
    out_ref[...] = (result * (z * jax.nn.sigmoid(z))).astype(BF16)


def _sb(p3, u):
    b, s, _ = p3.shape
    T = u.shape[0]
    nblk = SB_HEADS * SB_HEAD_DIM // LANES

    def qspec(col):
        return pl.BlockSpec((None, T, LANES), lambda bb, hp, i: (bb, i, col // LANES + hp))

    def kspec(col):
        return pl.BlockSpec((None, s, LANES), lambda bb, hp, i: (bb, 0, col // LANES + hp))

    return pl.pallas_call(
        functools.partial(_sb_kernel, T=T),
        grid=(b, nblk, s // T),
        in_specs=[qspec(C_SB_Q), kspec(C_SB_K), kspec(C_SB_V), qspec(C_SB_Z),
                  pl.BlockSpec((T, T), lambda bb, hp, i: (0, 0))],
        out_specs=pl.BlockSpec((None, T, LANES), lambda bb, hp, i: (bb, i, hp)),
        out_shape=jax.ShapeDtypeStruct((b, s, BRANCH_WIDTH), BF16),
        compiler_params=pltpu.CompilerParams(
            dimension_semantics=("arbitrary", "arbitrary", "arbitrary"),
            vmem_limit_bytes=VMEM_LIMIT),
        name="stick_breaking",
    )(p3, p3, p3, p3, u)


def _ml_kernel(q_ref, k_ref, v_ref, o_ref, z_ref, sm_ref, cwq_ref, cwk_ref, gb_ref, nw_ref, tri_ref,
               out_ref, xq_s, xk_s, c_s, n_s, m_s, *, L):
    c = pl.program_id(1)
    pad = 8

    @pl.when(c == 0)
    def _init():
        xq_s[0:pad, :] = jnp.zeros((pad, xq_s.shape[1]), F32)
        xk_s[0:pad, :] = jnp.zeros((pad, xk_s.shape[1]), F32)
        c_s[...] = jnp.zeros(c_s.shape, F32)
        n_s[...] = jnp.zeros(n_s.shape, F32)
        m_s[...] = jnp.zeros(m_s.shape, F32)

    def conv_silu(x_ref, xs, cw_ref):
        xs[pad:pad + L, :] = x_ref[...]
        w = cw_ref[...]
        y = xs[pad:pad + L, :] * w[ML_CONV - 1:ML_CONV, :]
        for j in range(1, ML_CONV):
            y = y + xs[pad - j:pad - j + L, :] * w[ML_CONV - 1 - j:ML_CONV - j, :]
        xs[0:pad, :] = xs[L:L + pad, :]
        return y * jax.nn.sigmoid(y)

    qc = conv_silu(q_ref, xq_s, cwq_ref)
    kc = conv_silu(k_ref, xk_s, cwk_ref) * (ML_HEAD_DIM ** -0.5)

    g = sm_ref[...] + gb_ref[...]
    lf = _log_sigmoid(g)
    tri = tri_ref[...]
    f_hi, f_mid, f_lo = _split3(lf)
    bcum = _dot(tri, f_hi) + (_dot(tri, f_mid) + _dot(tri, f_lo))
    g_t = g.T
    b_t = bcum.T
    rr = lax.broadcasted_iota(I32, (L, L), 0)
    cc = lax.broadcasted_iota(I32, (L, L), 1)
    tril = cc <= rr

    for h in range(ML_HEADS):
        hs = slice(h * ML_HEAD_DIM, (h + 1) * ML_HEAD_DIM)
        q = qc[:, hs]
        k = kc[:, hs]
        v = v_ref[:, hs]
        qb, kb, vb = q.astype(BF16), k.astype(BF16), v.astype(BF16)
        i_col = g[:, SM_MI + h:SM_MI + h + 1]
        b_col = bcum[:, SM_MF + h:SM_MF + h + 1]
        i_row = g_t[SM_MI + h:SM_MI + h + 1, :]
        b_row = b_t[SM_MF + h:SM_MF + h + 1, :]
        m_prev = m_s[h][:, 0:1]
        c_mem = c_s[h]
        n_row = n_s[h]

        log_d = b_col - b_row + i_row
        log_inter = b_col + m_prev
        m_row = jnp.maximum(jnp.max(jnp.where(tril, log_d, -jnp.inf), axis=1, keepdims=True),
                            log_inter)
        w_intra = jnp.where(tril, jnp.exp(log_d - m_row), 0.0)
        w_inter = jnp.exp(log_inter - m_row)
        sc = _dot_nt(qb, kb) * w_intra
        num = _dot(sc.astype(BF16), vb) + w_inter * _dot(qb, c_mem.astype(BF16))
        den = (jnp.sum(sc, axis=1, keepdims=True)
               + w_inter * jnp.sum(q * n_row, axis=1, keepdims=True))
        hval = num / jnp.maximum(jnp.abs(den), jnp.exp(-m_row))

        b_last = b_col[L - 1:L, :]
        log_w = b_last - b_col + i_col
        m_new = jnp.maximum(b_last + m_prev, jnp.max(log_w, axis=0, keepdims=True))
        w_upd = jnp.exp(log_w - m_new)
        decay = jnp.exp(b_last + m_prev - m_new)
        kw = k * w_upd
        c_s[h] = decay * c_mem + _dot(kw.T.astype(BF16), vb)
        n_s[h] = decay * n_row + jnp.sum(kw, axis=0, keepdims=True)
        m_s[h] = jnp.broadcast_to(m_new, (1, LANES))

        mu = jnp.mean(hval, axis=-1, keepdims=True)
        d = hval - mu
        var = jnp.mean(d * d, axis=-1, keepdims=True)
        y = d * lax.rsqrt(var + NORM_EPS) * nw_ref[:, hs]
        z = z_ref[:, hs]
        y = y * jax.nn.sigmoid(o_ref[:, hs]) * (z * jax.nn.sigmoid(z))
        out_ref[:, hs] = y.astype(BF16)


def _ml(p3, conv_w, gate_bias, norm_w, tri):
    b, s, _ = p3.shape
    L = tri.shape[0]
    w = ML_HEADS * ML_HEAD_DIM

    def pspec(width, col):
        return pl.BlockSpec((None, L, width), lambda bb, c: (bb, c, col // width))

    def full(shape):
        return pl.BlockSpec(shape, lambda bb, c: (0,) * len(shape))

    return pl.pallas_call(
        functools.partial(_ml_kernel, L=L),
        grid=(b, s // L),
        in_specs=[pspec(w, C_ML_Q), pspec(w, C_ML_K), pspec(w, C_ML_V), pspec(w, C_ML_O),
                  pspec(w, C_ML_Z), pspec(LANES, C_SMALL),
                  full((ML_CONV, w)), full((ML_CONV, w)), full((1, LANES)), full((1, w)),
                  full((L, L))],
        out_specs=pl.BlockSpec((None, L, w), lambda bb, c: (bb, c, 0)),
        out_shape=jax.ShapeDtypeStruct((b, s, BRANCH_WIDTH), BF16),
        scratch_shapes=[
            pltpu.VMEM((L + 8, w), F32), pltpu.VMEM((L + 8, w), F32),
            pltpu.VMEM((ML_HEADS, ML_HEAD_DIM, ML_HEAD_DIM), F32),
            pltpu.VMEM((ML_HEADS, 1, ML_HEAD_DIM), F32),
            pltpu.VMEM((ML_HEADS, 1, LANES), F32),
        ],
        compiler_params=pltpu.CompilerParams(
            dimension_semantics=("arbitrary", "arbitrary"), vmem_limit_bytes=VMEM_LIMIT),
        name="mlstm",
    )(p3, p3, p3, p3, p3, p3, conv_w[:, :w], conv_w[:, w:], gate_bias, norm_w.reshape(1, w), tri)


def _out_kernel(ya_ref, yb_ref, yc_ref, mg_ref, x_ref, wb_ref, wo_ref, fw_ref, o_ref, *, final):
    mixed = None
    for gi, y_ref in enumerate((ya_ref, yb_ref, yc_ref)):
        proj = _dot(y_ref[...], wb_ref[gi])
        gate = jax.nn.sigmoid(mg_ref[:, gi * D_MODEL:(gi + 1) * D_MODEL])
        mixed = gate * proj if mixed is None else mixed + gate * proj
    out = x_ref[...] + _dot(mixed.astype(BF16), wo_ref[...])
    if final:
        ms = jnp.mean(out * out, axis=-1, keepdims=True)
        out = out * lax.rsqrt(ms + NORM_EPS) * fw_ref[...]
    o_ref[...] = out


def _out_proj(ya, yb, yc, p2d, x2d, wb, wo, final_w, final):
    m = x2d.shape[0]
    tm = min(512, m)
    mw = N_BRANCHES * D_MODEL

    def rows(width, col=0):
        return pl.BlockSpec((tm, width), lambda i: (i, col // width))

    return pl.pallas_call(
        functools.partial(_out_kernel, final=final),
        grid=(m // tm,),
        in_specs=[rows(BRANCH_WIDTH), rows(BRANCH_WIDTH), rows(BRANCH_WIDTH), rows(mw, C_MERGE),
                  rows(D_MODEL),
                  pl.BlockSpec(wb.shape, lambda i: (0, 0, 0)),
                  pl.BlockSpec(wo.shape, lambda i: (0, 0)),
                  pl.BlockSpec((1, D_MODEL), lambda i: (0, 0))],
        out_specs=rows(D_MODEL),
        out_shape=jax.ShapeDtypeStruct((m, D_MODEL), F32),
        compiler_params=pltpu.CompilerParams(
            dimension_semantics=("arbitrary",), vmem_limit_bytes=VMEM_LIMIT),
        name="out_proj",
    )(ya, yb, yc, p2d, x2d, wb, wo, final_w.reshape(1, D_MODEL))


def _arrange_w_in(w_in):
    def c(a, b_):
        return w_in[..., a:b_]
    small_pad = jnp.zeros(w_in.shape[:-1] + (LANES - 92,), w_in.dtype)
    cols = [
        c(_O_DSA_Q, _O_CKV) * DSA_HEAD_DIM ** -0.5, c(_O_CKV, _O_KROPE),
        c(_O_IDX_K, _O_IDX_W), c(_O_KROPE, _O_IDX_Q), c(_O_IDX_W, _O_DSA_Z),
        c(_O_ML_I, _O_ML_F), c(_O_ML_F, _O_ML_O), small_pad,
        c(_O_IDX_Q, _O_IDX_K), c(_O_DSA_Z, _O_SB_Q),
        c(_O_SB_Q, _O_SB_K) * SB_HEAD_DIM ** -0.5, c(_O_SB_K, _O_ML_QK),
        c(_O_ML_QK, _O_ML_V), c(_O_ML_V, _O_ML_I), c(_O_ML_O, _O_MERGE), c(_O_MERGE, _O_END),
    ]
    return jnp.concatenate(cols, axis=-1).astype(BF16)


def _rope_tables(seq):
    pos = jnp.arange(seq, dtype=F32)
    inv = ROPE_THETA ** (-jnp.arange(0, ROPE_DIM, 2, dtype=F32) / ROPE_DIM)
    ang = pos[:, None] * inv[None, :]
    cos, sin = jnp.cos(ang), jnp.sin(ang)
    half = ROPE_DIM // 2
    r = jnp.arange(LANES) % DSA_HEAD_DIM
    cos_l = cos[:, r % half]
    sin_l = sin[:, r % half]
    cos_t = jnp.where(r < ROPE_DIM, cos_l, 1.0)
    sin_a = jnp.where(r < half, -sin_l, 0.0)
    sin_b = jnp.where((r >= half) & (r < ROPE_DIM), sin_l, 0.0)
    return cos_t, sin_a, sin_b


def _dsa_query_matrix(w_uk):
    rank, nh, nope = w_uk.shape
    wq = jnp.zeros((nh, DSA_HEAD_DIM, nh, 2 * LANES), F32)
    eye = jnp.eye(ROPE_DIM, dtype=F32)
    for h in range(nh):
        wq = wq.at[h, ROPE_DIM:, h, :rank].set(w_uk[:, h, :].T)
        wq = wq.at[h, :ROPE_DIM, h, LANES + SM_KR:LANES + SM_KR + ROPE_DIM].set(eye)
    return wq.reshape(nh * DSA_HEAD_DIM, nh * 2 * LANES).astype(BF16)


def _dsa_value_matrix(w_uv):
    rank, nh, vd = w_uv.shape
    wv = jnp.zeros((nh, rank, nh, vd), F32)
    for h in range(nh):
        wv = wv.at[h, :, h, :].set(w_uv[:, h, :])
    return wv.reshape(nh * rank, nh * vd).astype(BF16)


def kernel(x, w_in, w_dsa_uk, w_dsa_uv, dsa_kv_norm_w, ml_conv_w, ml_i_bias, ml_f_bias, ml_norm_w,
           w_branch, w_out, norm_w, final_norm_w):
    bsz, seq, _ = x.shape
    depth = w_in.shape[0]
    cos_t, sin_a, sin_b = _rope_tables(seq)
    w_arr = _arrange_w_in(w_in)
    wb = w_branch.astype(BF16)
    wo = w_out.astype(BF16)
    T = min(256, seq)
    idx = jnp.arange(T)
    u_after = (idx[:, None] > idx[None, :]).astype(BF16)
    tri_incl = (idx[None, :] <= idx[:, None]).astype(BF16)
    gate_bias = jnp.zeros((depth, 1, LANES), F32)
    gate_bias = gate_bias.at[:, 0, SM_MI:SM_MI + ML_HEADS].set(ml_i_bias)
    gate_bias = gate_bias.at[:, 0, SM_MF:SM_MF + ML_HEADS].set(ml_f_bias)

    x2d = x.reshape(bsz * seq, D_MODEL)
    for l in range(depth):
        p2d = _in_proj(x2d, norm_w[l], w_arr[l])
        p3 = p2d.reshape(bsz, seq, P_COLS)
        ya = _dsa(p3, cos_t, sin_a, sin_b, dsa_kv_norm_w[l],
                  _dsa_query_matrix(w_dsa_uk[l]), _dsa_value_matrix(w_dsa_uv[l]))
        yb = _sb(p3, u_after)
        yc = _ml(p3, ml_conv_w[l], gate_bias[l], ml_norm_w[l], tri_incl)
        x2d = _out_proj(ya.reshape(-1, BRANCH_WIDTH), yb.reshape(-1, BRANCH_WIDTH),
                        yc.reshape(-1, BRANCH_WIDTH), p2d, x2d, wb[l], wo[l], final_norm_w,
                        final=(l == depth - 1))
    return x2d.reshape(bsz, seq, D_MODEL)
```

```python
import functools

import jax
import jax.numpy as jnp
from jax import lax
from jax.experimental import pallas as pl
from jax.experimental.pallas import tpu as pltpu

F32 = jnp.float32
BF16 = jnp.bfloat16
I32 = jnp.int32

D_MODEL = 1024
ROPE_THETA = 500000.0
ROPE_DIM = 16
NORM_EPS = 1e-6
DSA_HEADS = 8
DSA_HEAD_DIM = 64
DSA_NOPE_DIM = DSA_HEAD_DIM - ROPE_DIM
DSA_KV_RANK = 128
DSA_V_DIM = 64
IDX_HEADS = 4
IDX_DIM = 64
INDEX_TOPK = 256
SB_HEADS = 8
SB_HEAD_DIM = 64
ML_HEADS = 4
ML_HEAD_DIM = 128
ML_CONV = 4
N_BRANCHES = 3
BRANCH_WIDTH = 512

LANES = 128
VMEM_LIMIT = 56 * 1024 * 1024

_O_DSA_Q, _O_CKV, _O_KROPE, _O_IDX_Q, _O_IDX_K, _O_IDX_W = 0, 512, 640, 656, 912, 976
_O_DSA_Z, _O_SB_Q, _O_SB_K, _O_SB_V, _O_SB_Z = 980, 1492, 2004, 2516, 3028
_O_ML_QK, _O_ML_V, _O_ML_I, _O_ML_F, _O_ML_O, _O_ML_Z, _O_MERGE, _O_END = (
    3540, 4564, 5076, 5080, 5084, 5596, 6108, 9180)

C_DSA_Q, C_CKV, C_SMALL, C_IDX_Q, C_DSA_Z = 0, 512, 640, 768, 1024
C_SB_Q, C_SB_K, C_SB_V, C_SB_Z = 1536, 2048, 2560, 3072
C_ML_Q, C_ML_K, C_ML_V, C_ML_O, C_ML_Z, C_MERGE, P_COLS = 3584, 4096, 4608, 5120, 5632, 6144, 9216
SM_IK, SM_KR, SM_IW, SM_MI, SM_MF = 0, 64, 80, 84, 88

KEY_NEG_INF = -2139095041
INT_MIN = -2147483648
MASK_NEG = -1e30


def _dot(a, b):
    return jnp.dot(a, b, preferred_element_type=F32)


def _dot_nt(a, b):
    return lax.dot_general(a, b, (((1,), (1,)), ((), ())), preferred_element_type=F32)


def _split2(x):
    hi = x.astype(BF16)
    lo = (x - hi.astype(F32)).astype(BF16)
    return hi, lo


def _split3(x):
    hi = x.astype(BF16)
    r = x - hi.astype(F32)
    mid = r.astype(BF16)
    lo = (r - mid.astype(F32)).astype(BF16)
    return hi, mid, lo


def _log_sigmoid(x):
    return jnp.minimum(x, 0.0) - jnp.log(1.0 + jnp.exp(-jnp.abs(x)))


def _rope(x, cos, sin_a, sin_b):
    w = x.shape[1]
    reps = w // LANES
    if reps > 1:
        cos = jnp.concatenate([cos] * reps, axis=1)
        sin_a = jnp.concatenate([sin_a] * reps, axis=1)
        sin_b = jnp.concatenate([sin_b] * reps, axis=1)
    half = ROPE_DIM // 2
    return x * cos + pltpu.roll(x, w - half, 1) * sin_a + pltpu.roll(x, half, 1) * sin_b


def _in_proj_kernel(x_ref, nw_ref, w_ref, o_ref, h_ref):
    @pl.when(pl.program_id(1) == 0)
    def _():
        x = x_ref[...]
        ms = jnp.mean(x * x, axis=-1, keepdims=True)
        h_ref[...] = (x * lax.rsqrt(ms + NORM_EPS) * nw_ref[...]).astype(BF16)

    o_ref[...] = _dot(h_ref[...], w_ref[...])


def _in_proj(x2d, norm_w, w_bf16):
    m = x2d.shape[0]
    tm = min(1024, m)
    tn = 1024
    return pl.pallas_call(
        _in_proj_kernel,
        grid=(m // tm, P_COLS // tn),
        in_specs=[
            pl.BlockSpec((tm, D_MODEL), lambda i, j: (i, 0)),
            pl.BlockSpec((1, D_MODEL), lambda i, j: (0, 0)),
            pl.BlockSpec((D_MODEL, tn), lambda i, j: (0, j)),
        ],
        out_specs=pl.BlockSpec((tm, tn), lambda i, j: (i, j)),
        out_shape=jax.ShapeDtypeStruct((m, P_COLS), F32),
        scratch_shapes=[pltpu.VMEM((tm, D_MODEL), BF16)],
        compiler_params=pltpu.CompilerParams(
            dimension_semantics=("arbitrary", "arbitrary"), vmem_limit_bytes=VMEM_LIMIT),
        name="in_proj",
    )(x2d, norm_w.reshape(1, D_MODEL), w_bf16)


def _sortable_key(s):
    s = jnp.where(s == 0.0, 0.0, s)
    bits = lax.bitcast_convert_type(s, I32)
    return bits ^ ((bits >> 31) & 0x7FFFFFFF)


def _dsa_kernel(q_ref, ckv_ref, sm_ref, iq_ref, z_ref, cos_ref, sa_ref, sb_ref, kvw_ref, wq_ref,
                wuv_ref, out_ref, kv_s, ik_s, sc_s, bias_s, qs_s, thr_s, js_s,
                m_s, acc_s, *, T, topk, seq):
    i = pl.program_id(1)
    nh = DSA_HEADS
    cos, sin_a, sin_b = cos_ref[...], sa_ref[...], sb_ref[...]
    lane = lax.broadcasted_iota(I32, (T, LANES), 1)

    sm = _rope(sm_ref[...], cos, sin_a, sin_b)
    ckv = ckv_ref[...]
    ckv_n = ckv * lax.rsqrt(jnp.mean(ckv * ckv, axis=-1, keepdims=True) + NORM_EPS) * kvw_ref[...]
    row0 = pl.multiple_of(i * T, T)
    kv_s[pl.ds(row0, T), 0:LANES] = ckv_n.astype(BF16)
    kr = jnp.where((lane >= SM_KR) & (lane < SM_KR + ROPE_DIM), sm, 0.0)
    kv_s[pl.ds(row0, T), LANES:2 * LANES] = kr.astype(BF16)
    ik = jnp.where(lane < IDX_DIM, sm, 0.0)
    ik_hi = ik.astype(BF16).astype(F32)
    ik_s[pl.ds(row0, T), 0:LANES] = (ik_hi + pltpu.roll(ik - ik_hi, IDX_DIM, 1)).astype(BF16)
    ik_s[pl.ds(row0, T), LANES:2 * LANES] = ik_hi.astype(BF16)

    iq = _rope(iq_ref[...], cos, sin_a, sin_b)
    iq_parts = []
    for h in range(IDX_HEADS):
        t = iq[:, (h // 2) * LANES:(h // 2 + 1) * LANES]
        if h % 2:
            t = pltpu.roll(t, IDX_DIM, 1)
        t = jnp.where(lane < IDX_DIM, t, 0.0)
        t_hi = t.astype(BF16).astype(F32)
        iq_parts.append(jnp.concatenate([t_hi + pltpu.roll(t_hi, IDX_DIM, 1), t - t_hi],
                                        axis=1).astype(BF16))
    w_cols = [sm[:, SM_IW + h:SM_IW + h + 1] for h in range(IDX_HEADS)]

    def score_tile(kt):
        r0 = pl.multiple_of(kt * T, T)
        kk = ik_s[pl.ds(r0, T), :]
        s = jnp.zeros((T, T), F32)
        for h in range(IDX_HEADS):
            s = s + w_cols[h] * jnp.maximum(_dot_nt(iq_parts[h], kk), 0.0)
        return _sortable_key(s)

    def score_body(kt, carry):
        sc_s[kt] = score_tile(kt)
        return carry

    lax.fori_loop(0, i, score_body, 0)
    rr = lax.broadcasted_iota(I32, (T, T), 0)
    cc = lax.broadcasted_iota(I32, (T, T), 1)
    sc_s[i] = jnp.where(cc <= rr, score_tile(i), KEY_NEG_INF)

    thr_s[...] = jnp.full((T, 1), KEY_NEG_INF + 1, I32)
    js_s[...] = jnp.full((T, 1), seq, I32)
    RH = LANES

    @pl.when((i + 1) * T > topk)
    def _select():
        for hh in range(T // RH):
            r_lo = hh * RH

            def count(pred):
                def body(kt, acc):
                    tile = sc_s[kt, r_lo:r_lo + RH, :]
                    m = jnp.where(pred(tile, kt), 1, 0).astype(I32)
                    for c in range(T // LANES):
                        acc = acc + m[:, c * LANES:(c + 1) * LANES]
                    return acc
                acc = lax.fori_loop(0, i + 1, body, jnp.zeros((RH, LANES), I32))
                return jnp.sum(acc, axis=1, keepdims=True)

            def bit_body(it, ans):
                cand = ans + lax.shift_left(jnp.int32(1), 31 - it)
                c = count(lambda tile, kt: tile >= cand)
                return jnp.where(c >= topk, cand, ans)

            ans = lax.fori_loop(0, 32, bit_body, jnp.full((RH, 1), INT_MIN, I32))
            n_gt = count(lambda tile, kt: tile > ans)
            n_ge = count(lambda tile, kt: tile >= ans)
            need = topk - n_gt
            thr_s[r_lo:r_lo + RH, :] = jnp.maximum(ans, KEY_NEG_INF + 1)

            @pl.when(jnp.max(n_ge) > topk)
            def _ties():
                col = lax.broadcasted_iota(I32, (RH, T), 1)

                def pos_body(it, pos):
                    cand = pos + lax.shift_left(jnp.int32(1), (seq.bit_length() - 2) - it)
                    c = count(lambda tile, kt: (tile == ans) & (col + kt * T < cand))
                    return jnp.where(c < need, cand, pos)

                pos = lax.fori_loop(0, seq.bit_length() - 1, pos_body, jnp.zeros((RH, 1), I32))
                js_s[r_lo:r_lo + RH, :] = jnp.where(ans == KEY_NEG_INF, seq, pos)

    thr = thr_s[...]
    js = js_s[...]

    def bias_body(kt, carry):
        key = sc_s[kt]
        sel = (key > thr) | ((key == thr) & (cc + kt * T <= js))
        bias_s[kt] = jnp.where(sel, 0.0, MASK_NEG)
        return carry

    lax.fori_loop(0, i + 1, bias_body, 0)

    qr = _rope(q_ref[...], cos, sin_a, sin_b).astype(BF16)
    qcat = _dot(qr, wq_ref[...]).astype(BF16)
    for h in range(nh):
        qs_s[h] = qcat[:, h * 2 * LANES:(h + 1) * 2 * LANES]
    m_s[...] = jnp.full((nh, T, LANES), MASK_NEG, F32)
    acc_s[...] = jnp.zeros((nh, T, 2 * LANES), F32)
    ones = jnp.ones((T, LANES), BF16)

    def att_body(kt, carry):
        r0 = pl.multiple_of(kt * T, T)
        kv = kv_s[pl.ds(r0, T), :]
        vo = jnp.concatenate([kv[:, 0:LANES], ones], axis=1)
        bias = bias_s[kt]
        for h in range(nh):
            lg = _dot_nt(qs_s[h], kv) + bias
            m_old = m_s[h]
            m_new = jnp.maximum(m_old, jnp.max(lg, axis=1, keepdims=True))
            alpha = jnp.exp(m_old - m_new)
            p = jnp.exp(lg - jnp.concatenate([m_new, m_new], axis=1))
            pv = _dot(p.astype(BF16), vo)
            acc_s[h] = jnp.concatenate([alpha, alpha], axis=1) * acc_s[h] + pv
            m_s[h] = m_new
        return carry

    lax.fori_loop(0, i + 1, att_body, 0)

    outs = []
    for h in range(nh):
        acc = acc_s[h]
        outs.append((acc[:, 0:LANES] / acc[:, LANES:2 * LANES]).astype(BF16))
    y = _dot(jnp.concatenate(outs, axis=1), wuv_ref[...])
    z = z_ref[...]
    out_ref[...] = (y * (z * jax.nn.sigmoid(z))).astype(BF16)


def _dsa(p3, cos, sin_a, sin_b, kv_norm_w, wq, wuv):
    b, s, _ = p3.shape
    T = min(256, s)
    topk = min(INDEX_TOPK, s // 4)
    nkt = s // T
    kern = functools.partial(_dsa_kernel, T=T, topk=topk, seq=s)

    def pspec(width, col):
        return pl.BlockSpec((None, T, width), lambda bb, i: (bb, i, col // width))

    def tspec():
        return pl.BlockSpec((T, LANES), lambda bb, i: (i, 0))

    return pl.pallas_call(
        kern,
        grid=(b, nkt),
        in_specs=[
            pspec(512, C_DSA_Q), pspec(LANES, C_CKV), pspec(LANES, C_SMALL), pspec(256, C_IDX_Q),
            pspec(512, C_DSA_Z), tspec(), tspec(), tspec(),
            pl.BlockSpec((1, DSA_KV_RANK), lambda bb, i: (0, 0)),
            pl.BlockSpec(wq.shape, lambda bb, i: (0, 0)),
            pl.BlockSpec(wuv.shape, lambda bb, i: (0, 0)),
        ],
        out_specs=pl.BlockSpec((None, T, BRANCH_WIDTH), lambda bb, i: (bb, i, 0)),
        out_shape=jax.ShapeDtypeStruct((b, s, BRANCH_WIDTH), BF16),
        scratch_shapes=[
            pltpu.VMEM((s, 2 * LANES), BF16),
            pltpu.VMEM((s, 2 * LANES), BF16),
            pltpu.VMEM((nkt, T, T), I32),
            pltpu.VMEM((nkt, T, T), F32),
            pltpu.VMEM((DSA_HEADS, T, 2 * LANES), BF16),
            pltpu.VMEM((T, 1), I32),
            pltpu.VMEM((T, 1), I32),
            pltpu.VMEM((DSA_HEADS, T, LANES), F32),
            pltpu.VMEM((DSA_HEADS, T, 2 * LANES), F32),
        ],
        compiler_params=pltpu.CompilerParams(
            dimension_semantics=("arbitrary", "arbitrary"), vmem_limit_bytes=VMEM_LIMIT),
        name="dsa",
    )(p3, p3, p3, p3, p3, cos, sin_a, sin_b, kv_norm_w.reshape(1, DSA_KV_RANK), wq, wuv)


def _sb_kernel(q_ref, k_ref, v_ref, z_ref, u_ref, out_ref, *, T):
    i = pl.program_id(2)
    lane = lax.broadcasted_iota(I32, (T, LANES), 1)
    rr = lax.broadcasted_iota(I32, (T, T), 0)
    cc = lax.broadcasted_iota(I32, (T, T), 1)
    strict = cc < rr
    q = q_ref[...]
    u = u_ref[...]
    nhb = LANES // SB_HEAD_DIM
    in_head = [(lane >= hh * SB_HEAD_DIM) & (lane < (hh + 1) * SB_HEAD_DIM) for hh in range(nhb)]
    qh = [jnp.where(in_head[hh], q, 0.0).astype(BF16) for hh in range(nhb)]

    def tile(kt, state, diag):
        r0 = pl.multiple_of(kt * T, T)
        k = k_ref[pl.ds(r0, T), :].astype(BF16)
        v = v_ref[pl.ds(r0, T), :].astype(BF16)
        out = []
        for hh in range(nhb):
            carry, acc = state[hh]
            zz = _dot_nt(qh[hh], k)
            lsz = _log_sigmoid(zz)
            ls = lsz - zz
            if diag:
                ls = jnp.where(strict, ls, 0.0)
            la = _dot(ls.astype(BF16), u)
            a = jnp.exp(lsz + la + carry)
            if diag:
                a = jnp.where(strict, a, 0.0)
            acc = acc + _dot(a.astype(BF16), v)
            carry = carry + jnp.sum(ls, axis=1, keepdims=True)
            out.append((carry, acc))
        return tuple(out)

    init = tuple((jnp.zeros((T, 1), F32), jnp.zeros((T, LANES), F32)) for _ in range(nhb))
    state = tile(i, init, True)
    state = lax.fori_loop(0, i, lambda j, st: tile(i - 1 - j, st, False), state)
    result = state[0][1]
    for hh in range(1, nhb):
        result = jnp.where(in_head[hh], state[hh][1], result)
    z = z_ref[...]
    out_ref[...] = (result * (z * jax.nn.sigmoid(z))).astype(BF16)


def _sb(p3, u):
    b, s, _ = p3.shape
    T = u.shape[0]
    nblk = SB_HEADS * SB_HEAD_DIM // LANES

    def qspec(col):
        return pl.BlockSpec((None, T, LANES), lambda bb, hp, i: (bb, i, col // LANES + hp))

    def kspec(col):
        return pl.BlockSpec((None, s, LANES), lambda bb, hp, i: (bb, 0, col // LANES + hp))

    return pl.pallas_call(
        functools.partial(_sb_kernel, T=T),
        grid=(b, nblk, s // T),
        in_specs=[qspec(C_SB_Q), kspec(C_SB_K), kspec(C_SB_V), qspec(C_SB_Z),
                  pl.BlockSpec((T, T), lambda bb, hp, i: (0, 0))],
        out_specs=pl.BlockSpec((None, T, LANES), lambda bb, hp, i: (bb, i, hp)),
        out_shape=jax.ShapeDtypeStruct((b, s, BRANCH_WIDTH), BF16),
        compiler_params=pltpu.CompilerParams(
            dimension_semantics=("arbitrary", "arbitrary", "arbitrary"),
            vmem_limit_bytes=VMEM_LIMIT),
        name="stick_breaking",
    )(p3, p3, p3, p3, u)


def _ml_kernel(q_ref, k_ref, v_ref, o_ref, z_ref, sm_ref, cwq_ref, cwk_ref, gb_ref, nw_ref, tri_ref,
               out_ref, xq_s, xk_s, c_s, n_s, m_s, *, L):
    c = pl.program_id(1)
    pad = 8

    @pl.when(c == 0)
    def _init():
        xq_s[0:pad, :] = jnp.zeros((pad, xq_s.shape[1]), F32)
        xk_s[0:pad, :] = jnp.zeros((pad, xk_s.shape[1]), F32)
        c_s[...] = jnp.zeros(c_s.shape, F32)
        n_s[...] = jnp.zeros(n_s.shape, F32)
        m_s[...] = jnp.zeros(m_s.shape, F32)

    def conv_silu(x_ref, xs, cw_ref):
        xs[pad:pad + L, :] = x_ref[...]
        w = cw_ref[...]
        y = xs[pad:pad + L, :] * w[ML_CONV - 1:ML_CONV, :]
        for j in range(1, ML_CONV):
            y = y + xs[pad - j:pad - j + L, :] * w[ML_CONV - 1 - j:ML_CONV - j, :]
        xs[0:pad, :] = xs[L:L + pad, :]
        return y * jax.nn.sigmoid(y)

    qc = conv_silu(q_ref, xq_s, cwq_ref)
    kc = conv_silu(k_ref, xk_s, cwk_ref) * (ML_HEAD_DIM ** -0.5)

    g = sm_ref[...] + gb_ref[...]
    lf = _log_sigmoid(g)
    tri = tri_ref[...]
    f_hi, f_mid, f_lo = _split3(lf)
    bcum = _dot(tri, f_hi) + (_dot(tri, f_mid) + _dot(tri, f_lo))
    g_t = g.T
    b_t = bcum.T
    rr = lax.broadcasted_iota(I32, (L, L), 0)
    cc = lax.broadcasted_iota(I32, (L, L), 1)
    tril = cc <= rr

    for h in range(ML_HEADS):
        hs = slice(h * ML_HEAD_DIM, (h + 1) * ML_HEAD_DIM)
        q = qc[:, hs]
        k = kc[:, hs]
        v = v_ref[:, hs]
        qb, kb, vb = q.astype(BF16), k.astype(BF16), v.astype(BF16)
        i_col = g[:, SM_MI + h:SM_MI + h + 1]
        b_col = bcum[:, SM_MF + h:SM_MF + h + 1]
        i_row = g_t[SM_MI + h:SM_MI + h + 1, :]
        b_row = b_t[SM_MF + h:SM_MF + h + 1, :]
        m_prev = m_s[h][:, 0:1]
        c_mem = c_s[h]
        n_row = n_s[h]

        log_d = b_col - b_row + i_row
        log_inter = b_col + m_prev
        m_row = jnp.maximum(jnp.max(jnp.where(tril, log_d, -jnp.inf), axis=1, keepdims=True),
                            log_inter)
        w_intra = jnp.where(tril, jnp.exp(log_d - m_row), 0.0)
        w_inter = jnp.exp(log_inter - m_row)
        sc = _dot_nt(qb, kb) * w_intra
        num = _dot(sc.astype(BF16), vb) + w_inter * _dot(qb, c_mem.astype(BF16))
        den = (jnp.sum(sc, axis=1, keepdims=True)
               + w_inter * jnp.sum(q * n_row, axis=1, keepdims=True))
        hval = num / jnp.maximum(jnp.abs(den), jnp.exp(-m_row))

        b_last = b_col[L - 1:L, :]
        log_w = b_last - b_col + i_col
        m_new = jnp.maximum(b_last + m_prev, jnp.max(log_w, axis=0, keepdims=True))
        w_upd = jnp.exp(log_w - m_new)
        decay = jnp.exp(b_last + m_prev - m_new)
        kw = k * w_upd
        c_s[h] = decay * c_mem + _dot(kw.T.astype(BF16), vb)
        n_s[h] = decay * n_row + jnp.sum(kw, axis=0, keepdims=True)
        m_s[h] = jnp.broadcast_to(m_new, (1, LANES))

        mu = jnp.mean(hval, axis=-1, keepdims=True)
        d = hval - mu
        var = jnp.mean(d * d, axis=-1, keepdims=True)
        y = d * lax.rsqrt(var + NORM_EPS) * nw_ref[:, hs]
        z = z_ref[:, hs]
        y = y * jax.nn.sigmoid(o_ref[:, hs]) * (z * jax.nn.sigmoid(z))
        out_ref[:, hs] = y.astype(BF16)


def _ml(p3, conv_w, gate_bias, norm_w, tri):
    b, s, _ = p3.shape
    L = tri.shape[0]
    w = ML_HEADS * ML_HEAD_DIM

    def pspec(width, col):
        return pl.BlockSpec((None, L, width), lambda bb, c: (bb, c, col // width))

    def full(shape):
        return pl.BlockSpec(shape, lambda bb, c: (0,) * len(shape))

    return pl.pallas_call(
        functools.partial(_ml_kernel, L=L),
        grid=(b, s // L),
        in_specs=[pspec(w, C_ML_Q), pspec(w, C_ML_K), pspec(w, C_ML_V), pspec(w, C_ML_O),
                  pspec(w, C_ML_Z), pspec(LANES, C_SMALL),
                  full((ML_CONV, w)), full((ML_CONV, w)), full((1, LANES)), full((1, w)),
                  full((L, L))],
        out_specs=pl.BlockSpec((None, L, w), lambda bb, c: (bb, c, 0)),
        out_shape=jax.ShapeDtypeStruct((b, s, BRANCH_WIDTH), BF16),
        scratch_shapes=[
            pltpu.VMEM((L + 8, w), F32), pltpu.VMEM((L + 8, w), F32),
            pltpu.VMEM((ML_HEADS, ML_HEAD_DIM, ML_HEAD_DIM), F32),
            pltpu.VMEM((ML_HEADS, 1, ML_HEAD_DIM), F32),
            pltpu.VMEM((ML_HEADS, 1, LANES), F32),
        ],
        compiler_params=pltpu.CompilerParams(
            dimension_semantics=("arbitrary", "arbitrary"), vmem_limit_bytes=VMEM_LIMIT),
        name="mlstm",
    )(p3, p3, p3, p3, p3, p3, conv_w[:, :w], conv_w[:, w:], gate_bias, norm_w.reshape(1, w), tri)


def _out_kernel(ya_ref, yb_ref, yc_ref, mg_ref, x_ref, wb_ref, wo_ref, fw_ref, o_ref, *, final):
    mixed = None
    for gi, y_ref in enumerate((ya_ref, yb_ref, yc_ref)):
        proj = _dot(y_ref[...], wb_ref[gi])
        gate = jax.nn.sigmoid(mg_ref[:, gi * D_MODEL:(gi + 1) * D_MODEL])
        mixed = gate * proj if mixed is None else mixed + gate * proj
    out = x_ref[...] + _dot(mixed.astype(BF16), wo_ref[...])
    if final:
        ms = jnp.mean(out * out, axis=-1, keepdims=True)
        out = out * lax.rsqrt(ms + NORM_EPS) * fw_ref[...]
    o_ref[...] = out


def _out_proj(ya, yb, yc, p2d, x2d, wb, wo, final_w, final):
    m = x2d.shape[0]
    tm = min(512, m)
    mw = N_BRANCHES * D_MODEL

    def rows(width, col=0):
        return pl.BlockSpec((tm, width), lambda i: (i, col // width))

    return pl.pallas_call(
        functools.partial(_out_kernel, final=final),
        grid=(m // tm,),
        in_specs=[rows(BRANCH_WIDTH), rows(BRANCH_WIDTH), rows(BRANCH_WIDTH), rows(mw, C_MERGE),
                  rows(D_MODEL),
                  pl.BlockSpec(wb.shape, lambda i: (0, 0, 0)),
                  pl.BlockSpec(wo.shape, lambda i: (0, 0)),
                  pl.BlockSpec((1, D_MODEL), lambda i: (0, 0))],
        out_specs=rows(D_MODEL),
        out_shape=jax.ShapeDtypeStruct((m, D_MODEL), F32),
        compiler_params=pltpu.CompilerParams(
            dimension_semantics=("arbitrary",), vmem_limit_bytes=VMEM_LIMIT),
        name="out_proj",
    )(ya, yb, yc, p2d, x2d, wb, wo, final_w.reshape(1, D_MODEL))


def _arrange_w_in(w_in):
    def c(a, b_):
        return w_in[..., a:b_]
    small_pad = jnp.zeros(w_in.shape[:-1] + (LANES - 92,), w_in.dtype)
    cols = [
        c(_O_DSA_Q, _O_CKV) * DSA_HEAD_DIM ** -0.5, c(_O_CKV, _O_KROPE),
        c(_O_IDX_K, _O_IDX_W), c(_O_KROPE, _O_IDX_Q), c(_O_IDX_W, _O_DSA_Z),
        c(_O_ML_I, _O_ML_F), c(_O_ML_F, _O_ML_O), small_pad,
        c(_O_IDX_Q, _O_IDX_K), c(_O_DSA_Z, _O_SB_Q),
        c(_O_SB_Q, _O_SB_K) * SB_HEAD_DIM ** -0.5, c(_O_SB_K, _O_ML_QK),
        c(_O_ML_QK, _O_ML_V), c(_O_ML_V, _O_ML_I), c(_O_ML_O, _O_MERGE), c(_O_MERGE, _O_END),
    ]
    return jnp.concatenate(cols, axis=-1).astype(BF16)


def _rope_tables(seq):
    pos = jnp.arange(seq, dtype=F32)
    inv = ROPE_THETA ** (-jnp.arange(0, ROPE_DIM, 2, dtype=F32) / ROPE_DIM)
    ang = pos[:, None] * inv[None, :]
    cos, sin = jnp.cos(ang), jnp.sin(ang)
    half = ROPE_DIM // 2
    r = jnp.arange(LANES) % DSA_HEAD_DIM
    cos_l = cos[:, r % half]
    sin_l = sin[:, r % half]
    cos_t = jnp.where(r < ROPE_DIM, cos_l, 1.0)
    sin_a = jnp.where(r < half, -sin_l, 0.0)
    sin_b = jnp.where((r >= half) & (r < ROPE_DIM), sin_l, 0.0)
    return cos_t, sin_a, sin_b


def _dsa_query_matrix(w_uk):
    rank, nh, nope = w_uk.shape
    wq = jnp.zeros((nh, DSA_HEAD_DIM, nh, 2 * LANES), F32)
    eye = jnp.eye(ROPE_DIM, dtype=F32)
    for h in range(nh):
        wq = wq.at[h, ROPE_DIM:, h, :rank].set(w_uk[:, h, :].T)
        wq = wq.at[h, :ROPE_DIM, h, LANES + SM_KR:LANES + SM_KR + ROPE_DIM].set(eye)
    return wq.reshape(nh * DSA_HEAD_DIM, nh * 2 * LANES).astype(BF16)


def _dsa_value_matrix(w_uv):
    rank, nh, vd = w_uv.shape
    wv = jnp.zeros((nh, rank, nh, vd), F32)
    for h in range(nh):
        wv = wv.at[h, :, h, :].set(w_uv[:, h, :])
    return wv.reshape(nh * rank, nh * vd).astype(BF16)


def kernel(x, w_in, w_dsa_uk, w_dsa_uv, dsa_kv_norm_w, ml_conv_w, ml_i_bias, ml_f_bias, ml_norm_w,
           w_branch, w_out, norm_w, final_norm_w):
    bsz, seq, _ = x.shape
    depth = w_in.shape[0]
    cos_t, sin_a, sin_b = _rope_tables(seq)
    w_arr = _arrange_w_in(w_in)
    wb = w_branch.astype(BF16)
    wo = w_out.astype(BF16)
    T = min(256, seq)
    idx = jnp.arange(T)
    u_after = (idx[:, None] > idx[None, :]).astype(BF16)
    tri_incl = (idx[None, :] <= idx[:, None]).astype(BF16)
    gate_bias = jnp.zeros((depth, 1, LANES), F32)
    gate_bias = gate_bias.at[:, 0, SM_MI:SM_MI + ML_HEADS].set(ml_i_bias)
    gate_bias = gate_bias.at[:, 0, SM_MF:SM_MF + ML_HEADS].set(ml_f_bias)

    x2d = x.reshape(bsz * seq, D_MODEL)
    for l in range(depth):
        p2d = _in_proj(x2d, norm_w[l], w_arr[l])
        p3 = p2d.reshape(bsz, seq, P_COLS)
        ya = _dsa(p3, cos_t, sin_a, sin_b, dsa_kv_norm_w[l],
                  _dsa_query_matrix(w_dsa_uk[l]), _dsa_value_matrix(w_dsa_uv[l]))
        yb = _sb(p3, u_after)
        yc = _ml(p3, ml_conv_w[l], gate_bias[l], ml_norm_w[l], tri_incl)
        x2d = _out_proj(ya.reshape(-1, BRANCH_WIDTH), yb.reshape(-1, BRANCH_WIDTH),
                        yc.reshape(-1, BRANCH_WIDTH), p2d, x2d, wb[l], wo[l], final_norm_w,
                        final=(l == depth - 1))
    return x2d.reshape(bsz, seq, D_MODEL)
```

```python
import functools

import jax
import jax.numpy as jnp
from jax import lax
from jax.experimental import pallas as pl
from jax.experimental.pallas import tpu as pltpu

F32 = jnp.float32
BF16 = jnp.bfloat16
I32 = jnp.int32

D_MODEL = 1024
ROPE_THETA = 500000.0
ROPE_DIM = 16
NORM_EPS = 1e-6
DSA_HEADS = 8
DSA_HEAD_DIM = 64
DSA_NOPE_DIM = DSA_HEAD_DIM - ROPE_DIM
DSA_KV_RANK = 128
DSA_V_DIM = 64
IDX_HEADS = 4
IDX_DIM = 64
INDEX_TOPK = 256
SB_HEADS = 8
SB_HEAD_DIM = 64
ML_HEADS = 4
ML_HEAD_DIM = 128
ML_CONV = 4
N_BRANCHES = 3
BRANCH_WIDTH = 512

LANES = 128
VMEM_LIMIT = 56 * 1024 * 1024

_O_DSA_Q, _O_CKV, _O_KROPE, _O_IDX_Q, _O_IDX_K, _O_IDX_W = 0, 512, 640, 656, 912, 976
_O_DSA_Z, _O_SB_Q, _O_SB_K, _O_SB_V, _O_SB_Z = 980, 1492, 2004, 2516, 3028
_O_ML_QK, _O_ML_V, _O_ML_I, _O_ML_F, _O_ML_O, _O_ML_Z, _O_MERGE, _O_END = (
    3540, 4564, 5076, 5080, 5084, 5596, 6108, 9180)

C_DSA_Q, C_CKV, C_SMALL, C_IDX_Q, C_DSA_Z = 0, 512, 640, 768, 1024
C_SB_Q, C_SB_K, C_SB_V, C_SB_Z = 1536, 2048, 2560, 3072
C_ML_Q, C_ML_K, C_ML_V, C_ML_O, C_ML_Z, C_MERGE, P_COLS = 3584, 4096, 4608, 5120, 5632, 6144, 9216
SM_IK, SM_KR, SM_IW, SM_MI, SM_MF = 0, 64, 80, 84, 88

KEY_NEG_INF = -2139095041
INT_MIN = -2147483648
MASK_NEG = -1e30
LOG2E = 1.4426950408889634
ONES_ROWS = 16


def _dot(a, b):
    return jnp.dot(a, b, preferred_element_type=F32)


def _dot_nt(a, b):
    return lax.dot_general(a, b, (((1,), (1,)), ((), ())), preferred_element_type=F32)


def _split2(x):
    hi = x.astype(BF16)
    lo = (x - hi.astype(F32)).astype(BF16)
    return hi, lo


def _split3(x):
    hi = x.astype(BF16)
    r = x - hi.astype(F32)
    mid = r.astype(BF16)
    lo = (r - mid.astype(F32)).astype(BF16)
    return hi, mid, lo


def _log_sigmoid(x):
    return jnp.minimum(x, 0.0) - jnp.log(1.0 + jnp.exp(-jnp.abs(x)))


def _rope(x, cos, sin_a, sin_b):
    w = x.shape[1]
    reps = w // LANES
    if reps > 1:
        cos = jnp.concatenate([cos] * reps, axis=1)
        sin_a = jnp.concatenate([sin_a] * reps, axis=1)
        sin_b = jnp.concatenate([sin_b] * reps, axis=1)
    half = ROPE_DIM // 2
    return x * cos + pltpu.roll(x, w - half, 1) * sin_a + pltpu.roll(x, half, 1) * sin_b


def _in_proj_kernel(x_ref, nw_ref, w_ref, o_ref, h_ref):
    @pl.when(pl.program_id(1) == 0)
    def _():
        x = x_ref[...]
        ms = jnp.mean(x * x, axis=-1, keepdims=True)
        h_ref[...] = (x * lax.rsqrt(ms + NORM_EPS) * nw_ref[...]).astype(BF16)

    o_ref[...] = _dot(h_ref[...], w_ref[...])


def _in_proj(x2d, norm_w, w_bf16):
    m = x2d.shape[0]
    tm = min(1024, m)
    tn = 1024
    return pl.pallas_call(
        _in_proj_kernel,
        grid=(m // tm, P_COLS // tn),
        in_specs=[
            pl.BlockSpec((tm, D_MODEL), lambda i, j: (i, 0)),
            pl.BlockSpec((1, D_MODEL), lambda i, j: (0, 0)),
            pl.BlockSpec((D_MODEL, tn), lambda i, j: (0, j)),
        ],
        out_specs=pl.BlockSpec((tm, tn), lambda i, j: (i, j)),
        out_shape=jax.ShapeDtypeStruct((m, P_COLS), F32),
        scratch_shapes=[pltpu.VMEM((tm, D_MODEL), BF16)],
        compiler_params=pltpu.CompilerParams(
            dimension_semantics=("arbitrary", "arbitrary"), vmem_limit_bytes=VMEM_LIMIT),
        name="in_proj",
    )(x2d, norm_w.reshape(1, D_MODEL), w_bf16)


def _sortable_key(s):
    s = jnp.where(s == 0.0, 0.0, s)
    bits = lax.bitcast_convert_type(s, I32)
    return bits ^ ((bits >> 31) & 0x7FFFFFFF)


def _dsa_kernel(q_ref, ckv_ref, sm_ref, iq_ref, z_ref, cos_ref, sa_ref, sb_ref, kvw_ref, wq_ref,
                wuv_ref, out_ref, kv_s, ik_s, sc_s, bias_s, qs_s, thr_s, js_s,
                m_s, acc_s, vt_s, *, T, topk, seq):
    i = pl.program_id(1)
    nh = DSA_HEADS
    cos, sin_a, sin_b = cos_ref[...], sa_ref[...], sb_ref[...]
    lane = lax.broadcasted_iota(I32, (T, LANES), 1)

    sm = _rope(sm_ref[...], cos, sin_a, sin_b)
    ckv = ckv_ref[...]
    ckv_n = ckv * lax.rsqrt(jnp.mean(ckv * ckv, axis=-1, keepdims=True) + NORM_EPS) * kvw_ref[...]
    row0 = pl.multiple_of(i * T, T)
    kv_s[pl.ds(row0, T), 0:LANES] = ckv_n.astype(BF16)
    kr = jnp.where((lane >= SM_KR) & (lane < SM_KR + ROPE_DIM), sm, 0.0)
    kv_s[pl.ds(row0, T), LANES:2 * LANES] = kr.astype(BF16)
    ik = jnp.where(lane < IDX_DIM, sm, 0.0)
    ik_hi = ik.astype(BF16).astype(F32)
    ik_s[pl.ds(row0, T), 0:LANES] = (ik_hi + pltpu.roll(ik - ik_hi, IDX_DIM, 1)).astype(BF16)
    ik_s[pl.ds(row0, T), LANES:2 * LANES] = ik_hi.astype(BF16)
    vt_s[i, 0:DSA_KV_RANK, :] = ckv_n.T.astype(BF16)
    vt_s[i, DSA_KV_RANK:DSA_KV_RANK + ONES_ROWS, :] = jnp.ones((ONES_ROWS, T), BF16)

    iq = _rope(iq_ref[...], cos, sin_a, sin_b)
    iq_parts = []
    for h in range(IDX_HEADS):
        t = iq[:, (h // 2) * LANES:(h // 2 + 1) * LANES]
        if h % 2:
            t = pltpu.roll(t, IDX_DIM, 1)
        t = jnp.where(lane < IDX_DIM, t, 0.0)
        t_hi = t.astype(BF16).astype(F32)
        iq_parts.append(jnp.concatenate([t_hi + pltpu.roll(t_hi, IDX_DIM, 1), t - t_hi],
                                        axis=1).astype(BF16))
    iq_all = jnp.concatenate(iq_parts, axis=0)
    sm_t = sm.T
    w_rows = [sm_t[SM_IW + h:SM_IW + h + 1, :] for h in range(IDX_HEADS)]

    def score_tile(kt):
        r0 = pl.multiple_of(kt * T, T)
        kk = ik_s[pl.ds(r0, T), :]
        d_all = _dot_nt(kk, iq_all)
        s = jnp.zeros((T, T), F32)
        for h in range(IDX_HEADS):
            s = s + w_rows[h] * jnp.maximum(d_all[:, h * T:(h + 1) * T], 0.0)
        return _sortable_key(s)

    def score_body(kt, carry):
        sc_s[kt] = score_tile(kt)
        return carry

    lax.fori_loop(0, i, score_body, 0)
    kpos = lax.broadcasted_iota(I32, (T, T), 0)
    qpos = lax.broadcasted_iota(I32, (T, T), 1)
    sc_s[i] = jnp.where(kpos <= qpos, score_tile(i), KEY_NEG_INF)

    thr_s[...] = jnp.full((1, T), KEY_NEG_INF + 1, I32)
    js_s[...] = jnp.full((1, T), seq, I32)

    @pl.when((i + 1) * T > topk)
    def _select():
        def count(pred):
            def body(kt, acc):
                m = jnp.where(pred(sc_s[kt], kt), 1, 0).astype(I32)
                return acc + jnp.sum(m.reshape(T // 8, 8, T), axis=0)
            acc = lax.fori_loop(0, i + 1, body, jnp.zeros((8, T), I32))
            return jnp.sum(acc, axis=0, keepdims=True)

        def bit_body(it, carry):
            ans, n_ge = carry
            cand = ans + lax.shift_left(jnp.int32(1), 31 - it)
            c = count(lambda tile, kt: tile >= cand)
            up = c >= topk
            return jnp.where(up, cand, ans), jnp.where(up, c, n_ge)

        ans, n_ge = lax.fori_loop(
            0, 32, bit_body,
            (jnp.full((1, T), INT_MIN, I32), jnp.full((1, T), (i + 1) * T, I32)))
        thr_s[...] = jnp.maximum(ans, KEY_NEG_INF + 1)

        @pl.when(jnp.max(n_ge) > topk)
        def _ties():
            need = topk - count(lambda tile, kt: tile > ans)

            def pos_body(it, pos):
                cand = pos + lax.shift_left(jnp.int32(1), (seq.bit_length() - 2) - it)
                c = count(lambda tile, kt: (tile == ans) & (kpos + kt * T < cand))
                return jnp.where(c < need, cand, pos)

            pos = lax.fori_loop(0, seq.bit_length() - 1, pos_body, jnp.zeros((1, T), I32))
            js_s[...] = jnp.where(ans == KEY_NEG_INF, seq, pos)

    thr = thr_s[...]
    js = js_s[...]

    def bias_body(kt, carry):
        key = sc_s[kt]
        sel = (key > thr) | ((key == thr) & (kpos + kt * T <= js))
        bias_s[kt] = jnp.where(sel, 0.0, MASK_NEG)
        return carry

    lax.fori_loop(0, i + 1, bias_body, 0)

    qr = _rope(q_ref[...], cos, sin_a, sin_b).astype(BF16)
    qcat = (_dot(qr, wq_ref[...]) * LOG2E).astype(BF16)
    for h in range(nh):
        qs_s[h] = qcat[:, h * 2 * LANES:(h + 1) * 2 * LANES]
    m_s[...] = jnp.full((nh, 1, T), MASK_NEG, F32)
    acc_s[...] = jnp.zeros((nh, DSA_KV_RANK + ONES_ROWS, T), F32)

    def att_body(kt, carry):
        r0 = pl.multiple_of(kt * T, T)
        kv = kv_s[pl.ds(r0, T), :]
        vt = vt_s[kt]
        bias = bias_s[kt]
        lg_all = _dot_nt(kv, qs_s[...].reshape(nh * T, 2 * LANES))
        for h in range(nh):
            lg = lg_all[:, h * T:(h + 1) * T] + bias
            m_old = m_s[h]
            m_new = jnp.maximum(m_old, jnp.max(lg, axis=0, keepdims=True))
            alpha = jnp.exp2(m_old - m_new)
            p = jnp.exp2(lg - m_new)
            acc_s[h] = alpha * acc_s[h] + _dot(vt, p.astype(BF16))
            m_s[h] = m_new
        return carry

    lax.fori_loop(0, i + 1, att_body, 0)

    outs = []
    for h in range(nh):
        acc = acc_s[h]
        outs.append((acc[0:DSA_KV_RANK, :] / acc[DSA_KV_RANK:DSA_KV_RANK + 1, :]).astype(BF16))
    y_t = _dot(wuv_ref[...], jnp.concatenate(outs, axis=0))
    z = z_ref[...]
    out_ref[...] = (y_t.T * (z * jax.nn.sigmoid(z))).astype(BF16)


def _dsa(p3, cos, sin_a, sin_b, kv_norm_w, wq, wuv):
    b, s, _ = p3.shape
    T = min(256, s)
    topk = min(INDEX_TOPK, s // 4)
    nkt = s // T
    kern = functools.partial(_dsa_kernel, T=T, topk=topk, seq=s)

    def pspec(width, col):
        return pl.BlockSpec((None, T, width), lambda bb, i: (bb, i, col // width))

    def tspec():
        return pl.BlockSpec((T, LANES), lambda bb, i: (i, 0))

    return pl.pallas_call(
        kern,
        grid=(b, nkt),
        in_specs=[
            pspec(512, C_DSA_Q), pspec(LANES, C_CKV), pspec(LANES, C_SMALL), pspec(256, C_IDX_Q),
            pspec(512, C_DSA_Z), tspec(), tspec(), tspec(),
            pl.BlockSpec((1, DSA_KV_RANK), lambda bb, i: (0, 0)),
            pl.BlockSpec(wq.shape, lambda bb, i: (0, 0)),
            pl.BlockSpec(wuv.shape, lambda bb, i: (0, 0)),
        ],
        out_specs=pl.BlockSpec((None, T, BRANCH_WIDTH), lambda bb, i: (bb, i, 0)),
        out_shape=jax.ShapeDtypeStruct((b, s, BRANCH_WIDTH), BF16),
        scratch_shapes=[
            pltpu.VMEM((s, 2 * LANES), BF16),
            pltpu.VMEM((s, 2 * LANES), BF16),
            pltpu.VMEM((nkt, T, T), I32),
            pltpu.VMEM((nkt, T, T), F32),
            pltpu.VMEM((DSA_HEADS, T, 2 * LANES), BF16),
            pltpu.VMEM((1, T), I32),
            pltpu.VMEM((1, T), I32),
            pltpu.VMEM((DSA_HEADS, 1, T), F32),
            pltpu.VMEM((DSA_HEADS, DSA_KV_RANK + ONES_ROWS, T), F32),
            pltpu.VMEM((nkt, DSA_KV_RANK + ONES_ROWS, T), BF16),
        ],
        compiler_params=pltpu.CompilerParams(
            dimension_semantics=("arbitrary", "arbitrary"), vmem_limit_bytes=VMEM_LIMIT),
        name="dsa",
    )(p3, p3, p3, p3, p3, cos, sin_a, sin_b, kv_norm_w.reshape(1, DSA_KV_RANK), wq, wuv)


def _sb_kernel(q_ref, k_ref, v_ref, z_ref, u_ref, out_ref, *, T):
    i = pl.program_id(2)
    lane = lax.broadcasted_iota(I32, (T, LANES), 1)
    rr = lax.broadcasted_iota(I32, (T, T), 0)
    cc = lax.broadcasted_iota(I32, (T, T), 1)
    strict = cc < rr
    q = q_ref[...]
    u = u_ref[...]
    nhb = LANES // SB_HEAD_DIM
    in_head = [(lane >= hh * SB_HEAD_DIM) & (lane < (hh + 1) * SB_HEAD_DIM) for hh in range(nhb)]
    qh = [jnp.where(in_head[hh], q, 0.0).astype(BF16) for hh in range(nhb)]

    def tiles(kts, state, diag):
        state = list(state)
        for kt in kts:
            r0 = pl.multiple_of(kt * T, T)
            k = k_ref[pl.ds(r0, T), :].astype(BF16)
            v = v_ref[pl.ds(r0, T), :].astype(BF16)
            for hh in range(nhb):
                carry, acc = state[hh]
                zz = _dot_nt(qh[hh], k)
                lsz = _log_sigmoid(zz)
                ls = lsz - zz
                if diag:
                    ls = jnp.where(strict, ls, 0.0)
                la = _dot(ls.astype(BF16), u)
                a = jnp.exp(lsz + la + carry)
                if diag:
                    a = jnp.where(strict, a, 0.0)
                acc = acc + _dot(a.astype(BF16), v)
                carry = carry + jnp.sum(ls, axis=1, keepdims=True)
                state[hh] = (carry, acc)
        return tuple(state)

    init = tuple((jnp.zeros((T, 1), F32), jnp.zeros((T, LANES), F32)) for _ in range(nhb))
    state = tiles((i,), init, True)
    state = lax.fori_loop(
        0, i // 2, lambda j, st: tiles((i - 1 - 2 * j, i - 2 - 2 * j), st, False), state)
    state = lax.cond(i % 2 == 1, lambda st: tiles((0,), st, False), lambda st: st, state)
    result = state[0][1]
    for hh in range(1, nhb):
        result = jnp.where(in_head[hh], state[hh][1], result)
    z = z_ref[...]
    out_ref[...] = (result * (z * jax.nn.sigmoid(z))).astype(BF16)


def _sb(p3, u):
    b, s, _ = p3.shape
    T = u.shape[0]
    nblk = SB_HEADS * SB_HEAD_DIM // LANES

    def qspec(col):
        return pl.BlockSpec((None, T, LANES), lambda bb, hp, i: (bb, i, col // LANES + hp))

    def kspec(col):
        return pl.BlockSpec((None, s, LANES), lambda bb, hp, i: (bb, 0, col // LANES + hp))

    return pl.pallas_call(
        functools.partial(_sb_kernel, T=T),
        grid=(b, nblk, s // T),
        in_specs=[qspec(C_SB_Q), kspec(C_SB_K), kspec(C_SB_V), qspec(C_SB_Z),
                  pl.BlockSpec((T, T), lambda bb, hp, i: (0, 0))],
        out_specs=pl.BlockSpec((None, T, LANES), lambda bb, hp, i: (bb, i, hp)),
        out_shape=jax.ShapeDtypeStruct((b, s, BRANCH_WIDTH), BF16),
        compiler_params=pltpu.CompilerParams(
            dimension_semantics=("arbitrary", "arbitrary", "arbitrary"),
            vmem_limit_bytes=VMEM_LIMIT),
        name="stick_breaking",
    )(p3, p3, p3, p3, u)


def _ml_kernel(q_ref, k_ref, v_ref, o_ref, z_ref, sm_ref, cwq_ref, cwk_ref, gb_ref, nw_ref, tri_ref,
               out_ref, xq_s, xk_s, c_s, n_s, m_s, *, L):
    c = pl.program_id(1)
    pad = 8

    @pl.when(c == 0)
    def _init():
        xq_s[0:pad, :] = jnp.zeros((pad, xq_s.shape[1]), F32)
        xk_s[0:pad, :] = jnp.zeros((pad, xk_s.shape[1]), F32)
        c_s[...] = jnp.zeros(c_s.shape, F32)
        n_s[...] = jnp.zeros(n_s.shape, F32)
        m_s[...] = jnp.zeros(m_s.shape, F32)

    def conv_silu(x_ref, xs, cw_ref):
        xs[pad:pad + L, :] = x_ref[...]
        w = cw_ref[...]
        y = xs[pad:pad + L, :] * w[ML_CONV - 1:ML_CONV, :]
        for j in range(1, ML_CONV):
            y = y + xs[pad - j:pad - j + L, :] * w[ML_CONV - 1 - j:ML_CONV - j, :]
        xs[0:pad, :] = xs[L:L + pad, :]
        return y * jax.nn.sigmoid(y)

    qc = conv_silu(q_ref, xq_s, cwq_ref)
    kc = conv_silu(k_ref, xk_s, cwk_ref) * (ML_HEAD_DIM ** -0.5)

    g = sm_ref[...] + gb_ref[...]
    lf = _log_sigmoid(g)
    tri = tri_ref[...]
    f_hi, f_mid, f_lo = _split3(lf)
    bcum = _dot(tri, f_hi) + (_dot(tri, f_mid) + _dot(tri, f_lo))
    g_t = g.T
    b_t = bcum.T
    rr = lax.broadcasted_iota(I32, (L, L), 0)
    cc = lax.broadcasted_iota(I32, (L, L), 1)
    tril = cc <= rr

    for h in range(ML_HEADS):
        hs = slice(h * ML_HEAD_DIM, (h + 1) * ML_HEAD_DIM)
        q = qc[:, hs]
        k = kc[:, hs]
        v = v_ref[:, hs]
        qb, kb, vb = q.astype(BF16), k.astype(BF16), v.astype(BF16)
        i_col = g[:, SM_MI + h:SM_MI + h + 1]
        b_col = bcum[:, SM_MF + h:SM_MF + h + 1]
        i_row = g_t[SM_MI + h:SM_MI + h + 1, :]
        b_row = b_t[SM_MF + h:SM_MF + h + 1, :]
        m_prev = m_s[h][:, 0:1]
        c_mem = c_s[h]
        n_row = n_s[h]

        log_d = b_col - b_row + i_row
        log_inter = b_col + m_prev
        m_row = jnp.maximum(jnp.max(jnp.where(tril, log_d, -jnp.inf), axis=1, keepdims=True),
                            log_inter)
        w_intra = jnp.where(tril, jnp.exp(log_d - m_row), 0.0)
        w_inter = jnp.exp(log_inter - m_row)
        sc = _dot_nt(qb, kb) * w_intra
        num = _dot(sc.astype(BF16), vb) + w_inter * _dot(qb, c_mem.astype(BF16))
        den = (jnp.sum(sc, axis=1, keepdims=True)
               + w_inter * jnp.sum(q * n_row, axis=1, keepdims=True))
        hval = num / jnp.maximum(jnp.abs(den), jnp.exp(-m_row))

        b_last = b_col[L - 1:L, :]
        log_w = b_last - b_col + i_col
        m_new = jnp.maximum(b_last + m_prev, jnp.max(log_w, axis=0, keepdims=True))
        w_upd = jnp.exp(log_w - m_new)
        decay = jnp.exp(b_last + m_prev - m_new)
        kw = k * w_upd
        c_s[h] = decay * c_mem + _dot(kw.T.astype(BF16), vb)
        n_s[h] = decay * n_row + jnp.sum(kw, axis=0, keepdims=True)
        m_s[h] = jnp.broadcast_to(m_new, (1, LANES))

        mu = jnp.mean(hval, axis=-1, keepdims=True)
        d = hval - mu
        var = jnp.mean(d * d, axis=-1, keepdims=True)
        y = d * lax.rsqrt(var + NORM_EPS) * nw_ref[:, hs]
        z = z_ref[:, hs]
        y = y * jax.nn.sigmoid(o_ref[:, hs]) * (z * jax.nn.sigmoid(z))
        out_ref[:, hs] = y.astype(BF16)


def _ml(p3, conv_w, gate_bias, norm_w, tri):
    b, s, _ = p3.shape
    L = tri.shape[0]
    w = ML_HEADS * ML_HEAD_DIM

    def pspec(width, col):
        return pl.BlockSpec((None, L, width), lambda bb, c: (bb, c, col // width))

    def full(shape):
        return pl.BlockSpec(shape, lambda bb, c: (0,) * len(shape))

    return pl.pallas_call(
        functools.partial(_ml_kernel, L=L),
        grid=(b, s // L),
        in_specs=[pspec(w, C_ML_Q), pspec(w, C_ML_K), pspec(w, C_ML_V), pspec(w, C_ML_O),
                  pspec(w, C_ML_Z), pspec(LANES, C_SMALL),
                  full((ML_CONV, w)), full((ML_CONV, w)), full((1, LANES)), full((1, w)),
                  full((L, L))],
        out_specs=pl.BlockSpec((None, L, w), lambda bb, c: (bb, c, 0)),
        out_shape=jax.ShapeDtypeStruct((b, s, BRANCH_WIDTH), BF16),
        scratch_shapes=[
            pltpu.VMEM((L + 8, w), F32), pltpu.VMEM((L + 8, w), F32),
            pltpu.VMEM((ML_HEADS, ML_HEAD_DIM, ML_HEAD_DIM), F32),
            pltpu.VMEM((ML_HEADS, 1, ML_HEAD_DIM), F32),
            pltpu.VMEM((ML_HEADS, 1, LANES), F32),
        ],
        compiler_params=pltpu.CompilerParams(
            dimension_semantics=("arbitrary", "arbitrary"), vmem_limit_bytes=VMEM_LIMIT),
        name="mlstm",
    )(p3, p3, p3, p3, p3, p3, conv_w[:, :w], conv_w[:, w:], gate_bias, norm_w.reshape(1, w), tri)


def _out_kernel(ya_ref, yb_ref, yc_ref, mg_ref, x_ref, wb_ref, wo_ref, fw_ref, o_ref, *, final):
    mixed = None
    for gi, y_ref in enumerate((ya_ref, yb_ref, yc_ref)):
        proj = _dot(y_ref[...], wb_ref[gi])
        gate = jax.nn.sigmoid(mg_ref[:, gi * D_MODEL:(gi + 1) * D_MODEL])
        mixed = gate * proj if mixed is None else mixed + gate * proj
    out = x_ref[...] + _dot(mixed.astype(BF16), wo_ref[...])
    if final:
        ms = jnp.mean(out * out, axis=-1, keepdims=True)
        out = out * lax.rsqrt(ms + NORM_EPS) * fw_ref[...]
    o_ref[...] = out


def _out_proj(ya, yb, yc, p2d, x2d, wb, wo, final_w, final):
    m = x2d.shape[0]
    tm = min(512, m)
    mw = N_BRANCHES * D_MODEL

    def rows(width, col=0):
        return pl.BlockSpec((tm, width), lambda i: (i, col // width))

    return pl.pallas_call(
        functools.partial(_out_kernel, final=final),
        grid=(m // tm,),
        in_specs=[rows(BRANCH_WIDTH), rows(BRANCH_WIDTH), rows(BRANCH_WIDTH), rows(mw, C_MERGE),
                  rows(D_MODEL),
                  pl.BlockSpec(wb.shape, lambda i: (0, 0, 0)),
                  pl.BlockSpec(wo.shape, lambda i: (0, 0)),
                  pl.BlockSpec((1, D_MODEL), lambda i: (0, 0))],
        out_specs=rows(D_MODEL),
        out_shape=jax.ShapeDtypeStruct((m, D_MODEL), F32),
        compiler_params=pltpu.CompilerParams(
            dimension_semantics=("arbitrary",), vmem_limit_bytes=VMEM_LIMIT),
        name="out_proj",
    )(ya, yb, yc, p2d, x2d, wb, wo, final_w.reshape(1, D_MODEL))


def _arrange_w_in(w_in):
    def c(a, b_):
        return w_in[..., a:b_]
    small_pad = jnp.zeros(w_in.shape[:-1] + (LANES - 92,), w_in.dtype)
    cols = [
        c(_O_DSA_Q, _O_CKV) * DSA_HEAD_DIM ** -0.5, c(_O_CKV, _O_KROPE),
        c(_O_IDX_K, _O_IDX_W), c(_O_KROPE, _O_IDX_Q), c(_O_IDX_W, _O_DSA_Z),
        c(_O_ML_I, _O_ML_F), c(_O_ML_F, _O_ML_O), small_pad,
        c(_O_IDX_Q, _O_IDX_K), c(_O_DSA_Z, _O_SB_Q),
        c(_O_SB_Q, _O_SB_K) * SB_HEAD_DIM ** -0.5, c(_O_SB_K, _O_ML_QK),
        c(_O_ML_QK, _O_ML_V), c(_O_ML_V, _O_ML_I), c(_O_ML_O, _O_MERGE), c(_O_MERGE, _O_END),
    ]
    return jnp.concatenate(cols, axis=-1).astype(BF16)


def _rope_tables(seq):
    pos = jnp.arange(seq, dtype=F32)
    inv = ROPE_THETA ** (-jnp.arange(0, ROPE_DIM, 2, dtype=F32) / ROPE_DIM)
    ang = pos[:, None] * inv[None, :]
    cos, sin = jnp.cos(ang), jnp.sin(ang)
    half = ROPE_DIM // 2
    r = jnp.arange(LANES) % DSA_HEAD_DIM
    cos_l = cos[:, r % half]
    sin_l = sin[:, r % half]
    cos_t = jnp.where(r < ROPE_DIM, cos_l, 1.0)
    sin_a = jnp.where(r < half, -sin_l, 0.0)
    sin_b = jnp.where((r >= half) & (r < ROPE_DIM), sin_l, 0.0)
    return cos_t, sin_a, sin_b


def _dsa_query_matrix(w_uk):
    rank, nh, nope = w_uk.shape
    wq = jnp.zeros((nh, DSA_HEAD_DIM, nh, 2 * LANES), F32)
    eye = jnp.eye(ROPE_DIM, dtype=F32)
    for h in range(nh):
        wq = wq.at[h, ROPE_DIM:, h, :rank].set(w_uk[:, h, :].T)
        wq = wq.at[h, :ROPE_DIM, h, LANES + SM_KR:LANES + SM_KR + ROPE_DIM].set(eye)
    return wq.reshape(nh * DSA_HEAD_DIM, nh * 2 * LANES).astype(BF16)


def _dsa_value_matrix(w_uv):
    rank, nh, vd = w_uv.shape
    wv = jnp.zeros((nh, rank, nh, vd), F32)
    for h in range(nh):
        wv = wv.at[h, :, h, :].set(w_uv[:, h, :])
    return wv.reshape(nh * rank, nh * vd).T.astype(BF16)


def kernel(x, w_in, w_dsa_uk, w_dsa_uv, dsa_kv_norm_w, ml_conv_w, ml_i_bias, ml_f_bias, ml_norm_w,
           w_branch, w_out, norm_w, final_norm_w):
    bsz, seq, _ = x.shape
    depth = w_in.shape[0]
    cos_t, sin_a, sin_b = _rope_tables(seq)
    w_arr = _arrange_w_in(w_in)
    wb = w_branch.astype(BF16)
    wo = w_out.astype(BF16)
    T = min(256, seq)
    idx = jnp.arange(T)
    u_after = (idx[:, None] > idx[None, :]).astype(BF16)
    tri_incl = (idx[None, :] <= idx[:, None]).astype(BF16)
    gate_bias = jnp.zeros((depth, 1, LANES), F32)
    gate_bias = gate_bias.at[:, 0, SM_MI:SM_MI + ML_HEADS].set(ml_i_bias)
    gate_bias = gate_bias.at[:, 0, SM_MF:SM_MF + ML_HEADS].set(ml_f_bias)

    x2d = x.reshape(bsz * seq, D_MODEL)
    for l in range(depth):
        p2d = _in_proj(x2d, norm_w[l], w_arr[l])
        p3 = p2d.reshape(bsz, seq, P_COLS)
        ya = _dsa(p3, cos_t, sin_a, sin_b, dsa_kv_norm_w[l],
                  _dsa_query_matrix(w_dsa_uk[l]), _dsa_value_matrix(w_dsa_uv[l]))
        yb = _sb(p3, u_after)
        yc = _ml(p3, ml_conv_w[l], gate_bias[l], ml_norm_w[l], tri_incl)
        x2d = _out_proj(ya.reshape(-1, BRANCH_WIDTH), yb.reshape(-1, BRANCH_WIDTH),
                        yc.reshape(-1, BRANCH_WIDTH), p2d, x2d, wb[l], wo[l], final_norm_w,
                        final=(l == depth - 1))
    return x2d.reshape(bsz, seq, D_MODEL)
```

```python
import functools

import jax
import jax.numpy as jnp
from jax import lax
from jax.experimental import pallas as pl
from jax.experimental.pallas import tpu as pltpu

F32 = jnp.float32
BF16 = jnp.bfloat16
I32 = jnp.int32

D_MODEL = 1024
ROPE_THETA = 500000.0
ROPE_DIM = 16
NORM_EPS = 1e-6
DSA_HEADS = 8
DSA_HEAD_DIM = 64
DSA_NOPE_DIM = DSA_HEAD_DIM - ROPE_DIM
DSA_KV_RANK = 128
DSA_V_DIM = 64
IDX_HEADS = 4
IDX_DIM = 64
INDEX_TOPK = 256
SB_HEADS = 8
SB_HEAD_DIM = 64
ML_HEADS = 4
ML_HEAD_DIM = 128
ML_CONV = 4
N_BRANCHES = 3
BRANCH_WIDTH = 512

LANES = 128
VMEM_LIMIT = 56 * 1024 * 1024

_O_DSA_Q, _O_CKV, _O_KROPE, _O_IDX_Q, _O_IDX_K, _O_IDX_W = 0, 512, 640, 656, 912, 976
_O_DSA_Z, _O_SB_Q, _O_SB_K, _O_SB_V, _O_SB_Z = 980, 1492, 2004, 2516, 3028
_O_ML_QK, _O_ML_V, _O_ML_I, _O_ML_F, _O_ML_O, _O_ML_Z, _O_MERGE, _O_END = (
    3540, 4564, 5076, 5080, 5084, 5596, 6108, 9180)

C_DSA_Q, C_CKV, C_SMALL, C_IDX_Q, C_DSA_Z = 0, 512, 640, 768, 1024
C_SB_Q, C_SB_K, C_SB_V, C_SB_Z = 1536, 2048, 2560, 3072
C_ML_Q, C_ML_K, C_ML_V, C_ML_O, C_ML_Z, C_MERGE, P_COLS = 3584, 4096, 4608, 5120, 5632, 6144, 9216
SM_IK, SM_KR, SM_IW, SM_MI, SM_MF = 0, 64, 80, 84, 88

KEY_NEG_INF = -2139095041
INT_MIN = -2147483648
MASK_NEG = -1e30
LOG2E = 1.4426950408889634
F32_EXP_UNDERFLOW = -104.0
ONES_ROWS = 16


def _dot(a, b):
    return jnp.dot(a, b, preferred_element_type=F32)


def _dot_nt(a, b):
    return lax.dot_general(a, b, (((1,), (1,)), ((), ())), preferred_element_type=F32)


def _split2(x):
    hi = x.astype(BF16)
    lo = (x - hi.astype(F32)).astype(BF16)
    return hi, lo


def _split3(x):
    hi = x.astype(BF16)
    r = x - hi.astype(F32)
    mid = r.astype(BF16)
    lo = (r - mid.astype(F32)).astype(BF16)
    return hi, mid, lo


def _log_sigmoid(x):
    return jnp.minimum(x, 0.0) - jnp.log(1.0 + jnp.exp(-jnp.abs(x)))


def _rope(x, cos, sin_a, sin_b):
    w = x.shape[1]
    reps = w // LANES
    if reps > 1:
        cos = jnp.concatenate([cos] * reps, axis=1)
        sin_a = jnp.concatenate([sin_a] * reps, axis=1)
        sin_b = jnp.concatenate([sin_b] * reps, axis=1)
    half = ROPE_DIM // 2
    return x * cos + pltpu.roll(x, w - half, 1) * sin_a + pltpu.roll(x, half, 1) * sin_b


def _in_proj_kernel(x_ref, nw_ref, w_ref, o_ref, h_ref):
    @pl.when(pl.program_id(1) == 0)
    def _():
        x = x_ref[...]
        ms = jnp.mean(x * x, axis=-1, keepdims=True)
        h_ref[...] = (x * lax.rsqrt(ms + NORM_EPS) * nw_ref[...]).astype(BF16)

    o_ref[...] = _dot(h_ref[...], w_ref[...])


def _in_proj(x2d, norm_w, w_bf16):
    m = x2d.shape[0]
    tm = min(1024, m)
    tn = 1024
    return pl.pallas_call(
        _in_proj_kernel,
        grid=(m // tm, P_COLS // tn),
        in_specs=[
            pl.BlockSpec((tm, D_MODEL), lambda i, j: (i, 0)),
            pl.BlockSpec((1, D_MODEL), lambda i, j: (0, 0)),
            pl.BlockSpec((D_MODEL, tn), lambda i, j: (0, j)),
        ],
        out_specs=pl.BlockSpec((tm, tn), lambda i, j: (i, j)),
        out_shape=jax.ShapeDtypeStruct((m, P_COLS), F32),
        scratch_shapes=[pltpu.VMEM((tm, D_MODEL), BF16)],
        compiler_params=pltpu.CompilerParams(
            dimension_semantics=("arbitrary", "arbitrary"), vmem_limit_bytes=VMEM_LIMIT),
        name="in_proj",
    )(x2d, norm_w.reshape(1, D_MODEL), w_bf16)


def _key_to_float(key):
    return lax.bitcast_convert_type(key ^ ((key >> 31) & 0x7FFFFFFF), F32)


def _dsa_kernel(q_ref, ckv_ref, sm_ref, iq_ref, z_ref, cos_ref, sa_ref, sb_ref, kvw_ref, wq_ref,
                wuv_ref, tri_ref, out_ref, kv_s, ik_s, sc_s, bias_s, qs_s, thr_s, need_s,
                m_s, acc_s, vt_s, *, T, topk, seq):
    i = pl.program_id(1)
    nh = DSA_HEADS
    cos, sin_a, sin_b = cos_ref[...], sa_ref[...], sb_ref[...]
    lane = lax.broadcasted_iota(I32, (T, LANES), 1)

    sm = _rope(sm_ref[...], cos, sin_a, sin_b)
    ckv = ckv_ref[...]
    ckv_n = ckv * lax.rsqrt(jnp.mean(ckv * ckv, axis=-1, keepdims=True) + NORM_EPS) * kvw_ref[...]
    row0 = pl.multiple_of(i * T, T)
    kv_s[pl.ds(row0, T), 0:LANES] = ckv_n.astype(BF16)
    kr = jnp.where((lane >= SM_KR) & (lane < SM_KR + ROPE_DIM), sm, 0.0)
    kv_s[pl.ds(row0, T), LANES:2 * LANES] = kr.astype(BF16)
    ik = jnp.where(lane < IDX_DIM, sm, 0.0)
    ik_hi = ik.astype(BF16).astype(F32)
    ik_s[pl.ds(row0, T), 0:LANES] = (ik_hi + pltpu.roll(ik - ik_hi, IDX_DIM, 1)).astype(BF16)
    ik_s[pl.ds(row0, T), LANES:2 * LANES] = ik_hi.astype(BF16)
    vt_s[i, 0:DSA_KV_RANK, :] = ckv_n.T.astype(BF16)
    vt_s[i, DSA_KV_RANK:DSA_KV_RANK + ONES_ROWS, :] = jnp.ones((ONES_ROWS, T), BF16)

    iq = _rope(iq_ref[...], cos, sin_a, sin_b)
    iq_parts = []
    for h in range(IDX_HEADS):
        t = iq[:, (h // 2) * LANES:(h // 2 + 1) * LANES]
        if h % 2:
            t = pltpu.roll(t, IDX_DIM, 1)
        t = jnp.where(lane < IDX_DIM, t, 0.0)
        t_hi = t.astype(BF16).astype(F32)
        iq_parts.append(jnp.concatenate([t_hi + pltpu.roll(t_hi, IDX_DIM, 1), t - t_hi],
                                        axis=1).astype(BF16))
    iq_all = jnp.concatenate(iq_parts, axis=0)
    sm_t = sm.T
    w_rows = [sm_t[SM_IW + h:SM_IW + h + 1, :] for h in range(IDX_HEADS)]

    def score_tile(kt):
        r0 = pl.multiple_of(kt * T, T)
        kk = ik_s[pl.ds(r0, T), :]
        d_all = _dot_nt(kk, iq_all)
        s = jnp.zeros((T, T), F32)
        for h in range(IDX_HEADS):
            s = s + w_rows[h] * jnp.maximum(d_all[:, h * T:(h + 1) * T], 0.0)
        return jnp.where(s == 0.0, 0.0, s)

    def score_body(kt, carry):
        sc_s[kt] = score_tile(kt)
        return carry

    lax.fori_loop(0, i, score_body, 0)
    kpos = lax.broadcasted_iota(I32, (T, T), 0)
    qpos = lax.broadcasted_iota(I32, (T, T), 1)
    sc_s[i] = jnp.where(kpos <= qpos, score_tile(i), -jnp.inf)

    thr_s[...] = jnp.full((1, T), KEY_NEG_INF + 1, I32)
    need_s[...] = jnp.full((1, T), seq, F32)

    @pl.when((i + 1) * T > topk)
    def _select():
        def count_ge(cand_key):
            cand = _key_to_float(cand_key)

            def body(kt, acc):
                m = jnp.where(sc_s[kt] >= cand, 1, 0).astype(I32)
                return acc + jnp.sum(m.reshape(T // 8, 8, T), axis=0)
            acc = lax.fori_loop(0, i + 1, body, jnp.zeros((8, T), I32))
            return jnp.sum(acc, axis=0, keepdims=True)

        def bit_body(it, ans):
            cand = ans + lax.shift_left(jnp.int32(1), 31 - it)
            return jnp.where(count_ge(cand) >= topk, cand, ans)

        ans = lax.fori_loop(0, 32, bit_body, jnp.full((1, T), INT_MIN, I32))
        thr = jnp.maximum(ans, KEY_NEG_INF + 1)
        thr_s[...] = thr
        need_s[...] = (topk - count_ge(thr + 1)).astype(F32)

    thr = _key_to_float(thr_s[...])
    need = need_s[...]
    tri = tri_ref[...]

    def bias_body(kt, run):
        key = sc_s[kt]
        tie = key == thr
        upto = _dot(tri, jnp.where(tie, 1.0, 0.0).astype(BF16)) + run
        sel = (key > thr) | (tie & (upto <= need))
        bias_s[kt] = jnp.where(sel, 0.0, MASK_NEG)
        return upto[T - 1:T, :]

    lax.fori_loop(0, i + 1, bias_body, jnp.zeros((1, T), F32))

    qr = _rope(q_ref[...], cos, sin_a, sin_b).astype(BF16)
    qcat = (_dot(qr, wq_ref[...]) * LOG2E).astype(BF16)
    for h in range(nh):
        qs_s[h] = qcat[:, h * 2 * LANES:(h + 1) * 2 * LANES]
    m_s[...] = jnp.full((nh, 1, T), MASK_NEG, F32)
    acc_s[...] = jnp.zeros((nh, DSA_KV_RANK + ONES_ROWS, T), F32)

    def att_body(kt, carry):
        r0 = pl.multiple_of(kt * T, T)
        kv = kv_s[pl.ds(r0, T), :]
        vt = vt_s[kt]
        bias = bias_s[kt]
        lg_all = _dot_nt(kv, qs_s[...].reshape(nh * T, 2 * LANES))
        for h in range(nh):
            lg = lg_all[:, h * T:(h + 1) * T] + bias
            m_old = m_s[h]
            m_new = jnp.maximum(m_old, jnp.max(lg, axis=0, keepdims=True))
            alpha = jnp.exp2(m_old - m_new)
            p = jnp.exp2(lg - m_new)
            acc_s[h] = alpha * acc_s[h] + _dot(vt, p.astype(BF16))
            m_s[h] = m_new
        return carry

    lax.fori_loop(0, i + 1, att_body, 0)

    outs = []
    for h in range(nh):
        acc = acc_s[h]
        outs.append((acc[0:DSA_KV_RANK, :] / acc[DSA_KV_RANK:DSA_KV_RANK + 1, :]).astype(BF16))
    y_t = _dot(wuv_ref[...], jnp.concatenate(outs, axis=0))
    z = z_ref[...]
    out_ref[...] = (y_t.T * (z * jax.nn.sigmoid(z))).astype(BF16)


def _dsa(p3, cos, sin_a, sin_b, kv_norm_w, wq, wuv, tri):
    b, s, _ = p3.shape
    T = tri.shape[0]
    topk = min(INDEX_TOPK, s // 4)
    nkt = s // T
    kern = functools.partial(_dsa_kernel, T=T, topk=topk, seq=s)

    def pspec(width, col):
        return pl.BlockSpec((None, T, width), lambda bb, i: (bb, i, col // width))

    def tspec():
        return pl.BlockSpec((T, LANES), lambda bb, i: (i, 0))

    return pl.pallas_call(
        kern,
        grid=(b, nkt),
        in_specs=[
            pspec(512, C_DSA_Q), pspec(LANES, C_CKV), pspec(LANES, C_SMALL), pspec(256, C_IDX_Q),
            pspec(512, C_DSA_Z), tspec(), tspec(), tspec(),
            pl.BlockSpec((1, DSA_KV_RANK), lambda bb, i: (0, 0)),
            pl.BlockSpec(wq.shape, lambda bb, i: (0, 0)),
            pl.BlockSpec(wuv.shape, lambda bb, i: (0, 0)),
            pl.BlockSpec((T, T), lambda bb, i: (0, 0)),
        ],
        out_specs=pl.BlockSpec((None, T, BRANCH_WIDTH), lambda bb, i: (bb, i, 0)),
        out_shape=jax.ShapeDtypeStruct((b, s, BRANCH_WIDTH), BF16),
        scratch_shapes=[
            pltpu.VMEM((s, 2 * LANES), BF16),
            pltpu.VMEM((s, 2 * LANES), BF16),
            pltpu.VMEM((nkt, T, T), F32),
            pltpu.VMEM((nkt, T, T), F32),
            pltpu.VMEM((DSA_HEADS, T, 2 * LANES), BF16),
            pltpu.VMEM((1, T), I32),
            pltpu.VMEM((1, T), F32),
            pltpu.VMEM((DSA_HEADS, 1, T), F32),
            pltpu.VMEM((DSA_HEADS, DSA_KV_RANK + ONES_ROWS, T), F32),
            pltpu.VMEM((nkt, DSA_KV_RANK + ONES_ROWS, T), BF16),
        ],
        compiler_params=pltpu.CompilerParams(
            dimension_semantics=("arbitrary", "arbitrary"), vmem_limit_bytes=VMEM_LIMIT),
        name="dsa",
    )(p3, p3, p3, p3, p3, cos, sin_a, sin_b, kv_norm_w.reshape(1, DSA_KV_RANK), wq, wuv, tri)


def _sb_kernel(q_ref, k_ref, v_ref, z_ref, u_ref, out_ref, *, T):
    i = pl.program_id(2)
    nhb = LANES // SB_HEAD_DIM
    lane = lax.broadcasted_iota(I32, (T, LANES), 1)
    in_head = [(lane >= hh * SB_HEAD_DIM) & (lane < (hh + 1) * SB_HEAD_DIM) for hh in range(nhb)]
    q = q_ref[...]
    u = u_ref[...]
    qs = jnp.concatenate([jnp.where(in_head[hh], q, 0.0) for hh in range(nhb)], axis=0).astype(BF16)
    rr = lax.broadcasted_iota(I32, (nhb * T, T), 0) & (T - 1)
    cc = lax.broadcasted_iota(I32, (nhb * T, T), 1)
    strict = cc < rr

    def tile(kt, carry, acc, diag):
        r0 = pl.multiple_of(kt * T, T)
        k = k_ref[pl.ds(r0, T), :].astype(BF16)
        v = v_ref[pl.ds(r0, T), :].astype(BF16)
        zz = _dot_nt(qs, k)
        lsz = _log_sigmoid(zz)
        ls = lsz - zz
        if diag:
            ls = jnp.where(strict, ls, 0.0)
        la = _dot(ls.astype(BF16), u)
        a = jnp.exp(lsz + la + carry)
        if diag:
            a = jnp.where(strict, a, 0.0)
        acc = acc + _dot(a.astype(BF16), v)
        carry = carry + jnp.sum(ls, axis=1, keepdims=True)
        return carry, acc

    carry, acc = tile(i, jnp.zeros((nhb * T, 1), F32), jnp.zeros((nhb * T, LANES), F32), True)

    def cond(st):
        j, _, _, cmax = st
        return (j < i) & (cmax > F32_EXP_UNDERFLOW)

    def body(st):
        j, carry, acc, _ = st
        carry, acc = tile(i - 1 - j, carry, acc, False)
        return j + 1, carry, acc, jnp.max(carry)

    _, _, acc, _ = lax.while_loop(cond, body, (jnp.int32(0), carry, acc, jnp.max(carry)))
    result = acc[0:T, :]
    for hh in range(1, nhb):
        result = jnp.where(in_head[hh], acc[hh * T:(hh + 1) * T, :], result)
    z = z_ref[...]
    out_ref[...] = (result * (z * jax.nn.sigmoid(z))).astype(BF16)


def _sb(p3, u):
    b, s, _ = p3.shape
    T = u.shape[0]
    nblk = SB_HEADS * SB_HEAD_DIM // LANES

    def qspec(col):
        return pl.BlockSpec((None, T, LANES), lambda bb, hp, i: (bb, i, col // LANES + hp))

    def kspec(col):
        return pl.BlockSpec((None, s, LANES), lambda bb, hp, i: (bb, 0, col // LANES + hp))

    return pl.pallas_call(
        functools.partial(_sb_kernel, T=T),
        grid=(b, nblk, s // T),
        in_specs=[qspec(C_SB_Q), kspec(C_SB_K), kspec(C_SB_V), qspec(C_SB_Z),
                  pl.BlockSpec((T, T), lambda bb, hp, i: (0, 0))],
        out_specs=pl.BlockSpec((None, T, LANES), lambda bb, hp, i: (bb, i, hp)),
        out_shape=jax.ShapeDtypeStruct((b, s, BRANCH_WIDTH), BF16),
        compiler_params=pltpu.CompilerParams(
            dimension_semantics=("arbitrary", "arbitrary", "arbitrary"),
            vmem_limit_bytes=VMEM_LIMIT),
        name="stick_breaking",
    )(p3, p3, p3, p3, u)


def _ml_kernel(q_ref, k_ref, v_ref, o_ref, z_ref, sm_ref, cwq_ref, cwk_ref, gb_ref, nw_ref, tri_ref,
               out_ref, xq_s, xk_s, c_s, n_s, m_s, *, L):
    c = pl.program_id(1)
    pad = 8

    @pl.when(c == 0)
    def _init():
        xq_s[0:pad, :] = jnp.zeros((pad, xq_s.shape[1]), F32)
        xk_s[0:pad, :] = jnp.zeros((pad, xk_s.shape[1]), F32)
        c_s[...] = jnp.zeros(c_s.shape, F32)
        n_s[...] = jnp.zeros(n_s.shape, F32)
        m_s[...] = jnp.zeros(m_s.shape, F32)

    def conv_silu(x_ref, xs, cw_ref):
        xs[pad:pad + L, :] = x_ref[...]
        w = cw_ref[...]
        y = xs[pad:pad + L, :] * w[ML_CONV - 1:ML_CONV, :]
        for j in range(1, ML_CONV):
            y = y + xs[pad - j:pad - j + L, :] * w[ML_CONV - 1 - j:ML_CONV - j, :]
        xs[0:pad, :] = xs[L:L + pad, :]
        return y * jax.nn.sigmoid(y)

    qc = conv_silu(q_ref, xq_s, cwq_ref)
    kc = conv_silu(k_ref, xk_s, cwk_ref) * (ML_HEAD_DIM ** -0.5)

    g = sm_ref[...] + gb_ref[...]
    lf = _log_sigmoid(g)
    tri = tri_ref[...]
    f_hi, f_mid, f_lo = _split3(lf)
    bcum = _dot(tri, f_hi) + (_dot(tri, f_mid) + _dot(tri, f_lo))
    g_t = g.T
    b_t = bcum.T
    rr = lax.broadcasted_iota(I32, (L, L), 0)
    cc = lax.broadcasted_iota(I32, (L, L), 1)
    tril = cc <= rr

    for h in range(ML_HEADS):
        hs = slice(h * ML_HEAD_DIM, (h + 1) * ML_HEAD_DIM)
        q = qc[:, hs]
        k = kc[:, hs]
        v = v_ref[:, hs]
        qb, kb, vb = q.astype(BF16), k.astype(BF16), v.astype(BF16)
        i_col = g[:, SM_MI + h:SM_MI + h + 1]
        b_col = bcum[:, SM_MF + h:SM_MF + h + 1]
        i_row = g_t[SM_MI + h:SM_MI + h + 1, :]
        b_row = b_t[SM_MF + h:SM_MF + h + 1, :]
        m_prev = m_s[h][:, 0:1]
        c_mem = c_s[h]
        n_row = n_s[h]

        log_d = b_col - b_row + i_row
        log_inter = b_col + m_prev
        m_row = jnp.maximum(jnp.max(jnp.where(tril, log_d, -jnp.inf), axis=1, keepdims=True),
                            log_inter)
        w_intra = jnp.where(tril, jnp.exp(log_d - m_row), 0.0)
        w_inter = jnp.exp(log_inter - m_row)
        sc = _dot_nt(qb, kb) * w_intra
        num = _dot(sc.astype(BF16), vb) + w_inter * _dot(qb, c_mem.astype(BF16))
        den = (jnp.sum(sc, axis=1, keepdims=True)
               + w_inter * jnp.sum(q * n_row, axis=1, keepdims=True))
        hval = num / jnp.maximum(jnp.abs(den), jnp.exp(-m_row))

        b_last = b_col[L - 1:L, :]
        log_w = b_last - b_col + i_col
        m_new = jnp.maximum(b_last + m_prev, jnp.max(log_w, axis=0, keepdims=True))
        w_upd = jnp.exp(log_w - m_new)
        decay = jnp.exp(b_last + m_prev - m_new)
        kw = k * w_upd
        c_s[h] = decay * c_mem + _dot(kw.T.astype(BF16), vb)
        n_s[h] = decay * n_row + jnp.sum(kw, axis=0, keepdims=True)
        m_s[h] = jnp.broadcast_to(m_new, (1, LANES))

        mu = jnp.mean(hval, axis=-1, keepdims=True)
        d = hval - mu
        var = jnp.mean(d * d, axis=-1, keepdims=True)
        y = d * lax.rsqrt(var + NORM_EPS) * nw_ref[:, hs]
        z = z_ref[:, hs]
        y = y * jax.nn.sigmoid(o_ref[:, hs]) * (z * jax.nn.sigmoid(z))
        out_ref[:, hs] = y.astype(BF16)


def _ml(p3, conv_w, gate_bias, norm_w, tri):
    b, s, _ = p3.shape
    L = tri.shape[0]
    w = ML_HEADS * ML_HEAD_DIM

    def pspec(width, col):
        return pl.BlockSpec((None, L, width), lambda bb, c: (bb, c, col // width))

    def full(shape):
        return pl.BlockSpec(shape, lambda bb, c: (0,) * len(shape))

    return pl.pallas_call(
        functools.partial(_ml_kernel, L=L),
        grid=(b, s // L),
        in_specs=[pspec(w, C_ML_Q), pspec(w, C_ML_K), pspec(w, C_ML_V), pspec(w, C_ML_O),
                  pspec(w, C_ML_Z), pspec(LANES, C_SMALL),
                  full((ML_CONV, w)), full((ML_CONV, w)), full((1, LANES)), full((1, w)),
                  full((L, L))],
        out_specs=pl.BlockSpec((None, L, w), lambda bb, c: (bb, c, 0)),
        out_shape=jax.ShapeDtypeStruct((b, s, BRANCH_WIDTH), BF16),
        scratch_shapes=[
            pltpu.VMEM((L + 8, w), F32), pltpu.VMEM((L + 8, w), F32),
            pltpu.VMEM((ML_HEADS, ML_HEAD_DIM, ML_HEAD_DIM), F32),
            pltpu.VMEM((ML_HEADS, 1, ML_HEAD_DIM), F32),
            pltpu.VMEM((ML_HEADS, 1, LANES), F32),
        ],
        compiler_params=pltpu.CompilerParams(
            dimension_semantics=("arbitrary", "arbitrary"), vmem_limit_bytes=VMEM_LIMIT),
        name="mlstm",
    )(p3, p3, p3, p3, p3, p3, conv_w[:, :w], conv_w[:, w:], gate_bias, norm_w.reshape(1, w), tri)


def _out_kernel(ya_ref, yb_ref, yc_ref, mg_ref, x_ref, wb_ref, wo_ref, fw_ref, o_ref, *, final):
    mixed = None
    for gi, y_ref in enumerate((ya_ref, yb_ref, yc_ref)):
        proj = _dot(y_ref[...], wb_ref[gi])
        gate = jax.nn.sigmoid(mg_ref[:, gi * D_MODEL:(gi + 1) * D_MODEL])
        mixed = gate * proj if mixed is None else mixed + gate * proj
    out = x_ref[...] + _dot(mixed.astype(BF16), wo_ref[...])
    if final:
        ms = jnp.mean(out * out, axis=-1, keepdims=True)
        out = out * lax.rsqrt(ms + NORM_EPS) * fw_ref[...]
    o_ref[...] = out


def _out_proj(ya, yb, yc, p2d, x2d, wb, wo, final_w, final):
    m = x2d.shape[0]
    tm = min(512, m)
    mw = N_BRANCHES * D_MODEL

    def rows(width, col=0):
        return pl.BlockSpec((tm, width), lambda i: (i, col // width))

    return pl.pallas_call(
        functools.partial(_out_kernel, final=final),
        grid=(m // tm,),
        in_specs=[rows(BRANCH_WIDTH), rows(BRANCH_WIDTH), rows(BRANCH_WIDTH), rows(mw, C_MERGE),
                  rows(D_MODEL),
                  pl.BlockSpec(wb.shape, lambda i: (0, 0, 0)),
                  pl.BlockSpec(wo.shape, lambda i: (0, 0)),
                  pl.BlockSpec((1, D_MODEL), lambda i: (0, 0))],
        out_specs=rows(D_MODEL),
        out_shape=jax.ShapeDtypeStruct((m, D_MODEL), F32),
        compiler_params=pltpu.CompilerParams(
            dimension_semantics=("arbitrary",), vmem_limit_bytes=VMEM_LIMIT),
        name="out_proj",
    )(ya, yb, yc, p2d, x2d, wb, wo, final_w.reshape(1, D_MODEL))


def _arrange_w_in(w_in):
    def c(a, b_):
        return w_in[..., a:b_]
    small_pad = jnp.zeros(w_in.shape[:-1] + (LANES - 92,), w_in.dtype)
    cols = [
        c(_O_DSA_Q, _O_CKV) * DSA_HEAD_DIM ** -0.5, c(_O_CKV, _O_KROPE),
        c(_O_IDX_K, _O_IDX_W), c(_O_KROPE, _O_IDX_Q), c(_O_IDX_W, _O_DSA_Z),
        c(_O_ML_I, _O_ML_F), c(_O_ML_F, _O_ML_O), small_pad,
        c(_O_IDX_Q, _O_IDX_K), c(_O_DSA_Z, _O_SB_Q),
        c(_O_SB_Q, _O_SB_K) * SB_HEAD_DIM ** -0.5, c(_O_SB_K, _O_ML_QK),
        c(_O_ML_QK, _O_ML_V), c(_O_ML_V, _O_ML_I), c(_O_ML_O, _O_MERGE), c(_O_MERGE, _O_END),
    ]
    return jnp.concatenate(cols, axis=-1)


def _rope_tables(seq):
    pos = jnp.arange(seq, dtype=F32)
    inv = ROPE_THETA ** (-jnp.arange(0, ROPE_DIM, 2, dtype=F32) / ROPE_DIM)
    ang = pos[:, None] * inv[None, :]
    cos, sin = jnp.cos(ang), jnp.sin(ang)
    half = ROPE_DIM // 2
    r = jnp.arange(LANES) % DSA_HEAD_DIM
    cos_l = cos[:, r % half]
    sin_l = sin[:, r % half]
    cos_t = jnp.where(r < ROPE_DIM, cos_l, 1.0)
    sin_a = jnp.where(r < half, -sin_l, 0.0)
    sin_b = jnp.where((r >= half) & (r < ROPE_DIM), sin_l, 0.0)
    return cos_t, sin_a, sin_b


def _dsa_query_matrix(w_uk):
    rank, nh, nope = w_uk.shape
    wq = jnp.zeros((nh, DSA_HEAD_DIM, nh, 2 * LANES), F32)
    eye = jnp.eye(ROPE_DIM, dtype=F32)
    for h in range(nh):
        wq = wq.at[h, ROPE_DIM:, h, :rank].set(w_uk[:, h, :].T)
        wq = wq.at[h, :ROPE_DIM, h, LANES + SM_KR:LANES + SM_KR + ROPE_DIM].set(eye)
    return wq.reshape(nh * DSA_HEAD_DIM, nh * 2 * LANES).astype(BF16)


def _dsa_value_matrix(w_uv):
    rank, nh, vd = w_uv.shape
    wv = jnp.zeros((nh, rank, nh, vd), F32)
    for h in range(nh):
        wv = wv.at[h, :, h, :].set(w_uv[:, h, :])
    return wv.reshape(nh * rank, nh * vd).T.astype(BF16)


def kernel(x, w_in, w_dsa_uk, w_dsa_uv, dsa_kv_norm_w, ml_conv_w, ml_i_bias, ml_f_bias, ml_norm_w,
           w_branch, w_out, norm_w, final_norm_w):
    bsz, seq, _ = x.shape
    depth = w_in.shape[0]
    cos_t, sin_a, sin_b = _rope_tables(seq)
    w_arr = _arrange_w_in(w_in.astype(BF16))
    wb = w_branch.astype(BF16)
    wo = w_out.astype(BF16)
    T = min(256, seq)
    idx = jnp.arange(T)
    u_after = (idx[:, None] > idx[None, :]).astype(BF16)
    tri_incl = (idx[None, :] <= idx[:, None]).astype(BF16)
    gate_bias = jnp.zeros((depth, 1, LANES), F32)
    gate_bias = gate_bias.at[:, 0, SM_MI:SM_MI + ML_HEADS].set(ml_i_bias)
    gate_bias = gate_bias.at[:, 0, SM_MF:SM_MF + ML_HEADS].set(ml_f_bias)

    x2d = x.reshape(bsz * seq, D_MODEL)
    for l in range(depth):
        p2d = _in_proj(x2d, norm_w[l], w_arr[l])
        p3 = p2d.reshape(bsz, seq, P_COLS)
        ya = _dsa(p3, cos_t, sin_a, sin_b, dsa_kv_norm_w[l],
                  _dsa_query_matrix(w_dsa_uk[l]), _dsa_value_matrix(w_dsa_uv[l]), tri_incl)
        yb = _sb(p3, u_after)
        yc = _ml(p3, ml_conv_w[l], gate_bias[l], ml_norm_w[l], tri_incl)
        x2d = _out_proj(ya.reshape(-1, BRANCH_WIDTH), yb.reshape(-1, BRANCH_WIDTH),
                        yc.reshape(-1, BRANCH_WIDTH), p2d, x2d, wb[l], wo[l], final_norm_w,
                        final=(l == depth - 1))
    return x2d.reshape(bsz, seq, D_MODEL)
```

```python
import functools

import jax
import jax.numpy as jnp
from jax import lax
from jax.experimental import pallas as pl
from jax.experimental.pallas import tpu as pltpu

F32 = jnp.float32
BF16 = jnp.bfloat16
I32 = jnp.int32

D_MODEL = 1024
ROPE_THETA = 500000.0
ROPE_DIM = 16
NORM_EPS = 1e-6
DSA_HEADS = 8
DSA_HEAD_DIM = 64
DSA_NOPE_DIM = DSA_HEAD_DIM - ROPE_DIM
DSA_KV_RANK = 128
DSA_V_DIM = 64
IDX_HEADS = 4
IDX_DIM = 64
INDEX_TOPK = 256
SB_HEADS = 8
SB_HEAD_DIM = 64
ML_HEADS = 4
ML_HEAD_DIM = 128
ML_CONV = 4
N_BRANCHES = 3
BRANCH_WIDTH = 512

LANES = 128
VMEM_LIMIT = 56 * 1024 * 1024

_O_DSA_Q, _O_CKV, _O_KROPE, _O_IDX_Q, _O_IDX_K, _O_IDX_W = 0, 512, 640, 656, 912, 976
_O_DSA_Z, _O_SB_Q, _O_SB_K, _O_SB_V, _O_SB_Z = 980, 1492, 2004, 2516, 3028
_O_ML_QK, _O_ML_V, _O_ML_I, _O_ML_F, _O_ML_O, _O_ML_Z, _O_MERGE, _O_END = (
    3540, 4564, 5076, 5080, 5084, 5596, 6108, 9180)

F_CKV, F_SMALL, F_IDX_Q, PF_COLS = 0, 128, 256, 512
C_MERGE, C_DSA_Q, C_DSA_Z = 0, 3072, 3584
C_SB_Q, C_SB_K, C_SB_V, C_SB_Z = 4096, 4608, 5120, 5632
C_ML_Q, C_ML_K, C_ML_V, C_ML_O, C_ML_Z, PB_COLS = 6144, 6656, 7168, 7680, 8192, 8704
SM_IK, SM_KR, SM_IW, SM_MI, SM_MF = 0, 64, 80, 84, 88

KEY_NEG_INF = -2139095041
INT_MIN = -2147483648
MASK_NEG = -1e30
LOG2E = 1.4426950408889634
F32_EXP_UNDERFLOW = -104.0
ONES_ROWS = 16


def _dot(a, b):
    return jnp.dot(a, b, preferred_element_type=F32)


def _dot_nt(a, b):
    return lax.dot_general(a, b, (((1,), (1,)), ((), ())), preferred_element_type=F32)


def _split2(x):
    hi = x.astype(BF16)
    lo = (x - hi.astype(F32)).astype(BF16)
    return hi, lo


def _split3(x):
    hi = x.astype(BF16)
    r = x - hi.astype(F32)
    mid = r.astype(BF16)
    lo = (r - mid.astype(F32)).astype(BF16)
    return hi, mid, lo


def _log_sigmoid(x):
    return jnp.minimum(x, 0.0) - jnp.log(1.0 + jnp.exp(-jnp.abs(x)))


def _rope(x, cos, sin_a, sin_b):
    w = x.shape[1]
    reps = w // LANES
    if reps > 1:
        cos = jnp.concatenate([cos] * reps, axis=1)
        sin_a = jnp.concatenate([sin_a] * reps, axis=1)
        sin_b = jnp.concatenate([sin_b] * reps, axis=1)
    half = ROPE_DIM // 2
    return x * cos + pltpu.roll(x, w - half, 1) * sin_a + pltpu.roll(x, half, 1) * sin_b


def _in_proj_kernel(x_ref, nw_ref, w_ref, o_ref, h_ref):
    @pl.when(pl.program_id(1) == 0)
    def _():
        x = x_ref[...]
        ms = jnp.mean(x * x, axis=-1, keepdims=True)
        h_ref[...] = (x * lax.rsqrt(ms + NORM_EPS) * nw_ref[...]).astype(BF16)

    o_ref[...] = _dot(h_ref[...], w_ref[...]).astype(o_ref.dtype)


def _in_proj(x2d, norm_w, w_all, layer, tn, out_dtype, name):
    m = x2d.shape[0]
    n = w_all.shape[-1]
    tm = min(1024, m)
    return pl.pallas_call(
        _in_proj_kernel,
        grid=(m // tm, n // tn),
        in_specs=[
            pl.BlockSpec((tm, D_MODEL), lambda i, j: (i, 0)),
            pl.BlockSpec((1, D_MODEL), lambda i, j: (0, 0)),
            pl.BlockSpec((None, D_MODEL, tn), lambda i, j: (layer, 0, j)),
        ],
        out_specs=pl.BlockSpec((tm, tn), lambda i, j: (i, j)),
        out_shape=jax.ShapeDtypeStruct((m, n), out_dtype),
        scratch_shapes=[pltpu.VMEM((tm, D_MODEL), BF16)],
        compiler_params=pltpu.CompilerParams(
            dimension_semantics=("arbitrary", "arbitrary"), vmem_limit_bytes=VMEM_LIMIT),
        name=name,
    )(x2d, norm_w.reshape(1, D_MODEL), w_all)


def _key_to_float(key):
    return lax.bitcast_convert_type(key ^ ((key >> 31) & 0x7FFFFFFF), F32)


def _dsa_kernel(q_ref, ckv_ref, sm_ref, iq_ref, z_ref, cos_ref, sa_ref, sb_ref, kvw_ref, wq_ref,
                wuv_ref, tri_ref, out_ref, kv_s, ik_s, sc_s, bias_s, qs_s, thr_s, need_s,
                m_s, acc_s, vt_s, *, T, topk, seq):
    i = pl.program_id(1)
    nh = DSA_HEADS
    cos, sin_a, sin_b = cos_ref[...], sa_ref[...], sb_ref[...]
    lane = lax.broadcasted_iota(I32, (T, LANES), 1)

    sm = _rope(sm_ref[...], cos, sin_a, sin_b)
    ckv = ckv_ref[...]
    ckv_n = ckv * lax.rsqrt(jnp.mean(ckv * ckv, axis=-1, keepdims=True) + NORM_EPS) * kvw_ref[...]
    row0 = pl.multiple_of(i * T, T)
    kv_s[pl.ds(row0, T), 0:LANES] = ckv_n.astype(BF16)
    kr = jnp.where((lane >= SM_KR) & (lane < SM_KR + ROPE_DIM), sm, 0.0)
    kv_s[pl.ds(row0, T), LANES:2 * LANES] = kr.astype(BF16)
    ik = jnp.where(lane < IDX_DIM, sm, 0.0)
    ik_hi = ik.astype(BF16).astype(F32)
    ik_s[pl.ds(row0, T), 0:LANES] = (ik_hi + pltpu.roll(ik - ik_hi, IDX_DIM, 1)).astype(BF16)
    ik_s[pl.ds(row0, T), LANES:2 * LANES] = ik_hi.astype(BF16)
    vt_s[i, 0:DSA_KV_RANK, :] = ckv_n.T.astype(BF16)
    vt_s[i, DSA_KV_RANK:DSA_KV_RANK + ONES_ROWS, :] = jnp.ones((ONES_ROWS, T), BF16)

    iq = _rope(iq_ref[...], cos, sin_a, sin_b)
    iq_parts = []
    for h in range(IDX_HEADS):
        t = iq[:, (h // 2) * LANES:(h // 2 + 1) * LANES]
        if h % 2:
            t = pltpu.roll(t, IDX_DIM, 1)
        t = jnp.where(lane < IDX_DIM, t, 0.0)
        t_hi = t.astype(BF16).astype(F32)
        iq_parts.append(jnp.concatenate([t_hi + pltpu.roll(t_hi, IDX_DIM, 1), t - t_hi],
                                        axis=1).astype(BF16))
    iq_all = jnp.concatenate(iq_parts, axis=0)
    sm_t = sm.T
    w_rows = [sm_t[SM_IW + h:SM_IW + h + 1, :] for h in range(IDX_HEADS)]

    def score_tile(kt):
        r0 = pl.multiple_of(kt * T, T)
        kk = ik_s[pl.ds(r0, T), :]
        d_all = _dot_nt(kk, iq_all)
        s = jnp.zeros((T, T), F32)
        for h in range(IDX_HEADS):
            s = s + w_rows[h] * jnp.maximum(d_all[:, h * T:(h + 1) * T], 0.0)
        return jnp.where(s == 0.0, 0.0, s)

    def score_body(kt, carry):
        sc_s[kt] = score_tile(kt)
        return carry

    lax.fori_loop(0, i, score_body, 0)
    kpos = lax.broadcasted_iota(I32, (T, T), 0)
    qpos = lax.broadcasted_iota(I32, (T, T), 1)
    sc_s[i] = jnp.where(kpos <= qpos, score_tile(i), -jnp.inf)

    thr_s[...] = jnp.full((1, T), KEY_NEG_INF + 1, I32)
    need_s[...] = jnp.full((1, T), seq, F32)

    @pl.when((i + 1) * T > topk)
    def _select():
        def count_ge(cand_key):
            cand = _key_to_float(cand_key)

            def body(kt, acc):
                m = jnp.where(sc_s[kt] >= cand, 1, 0).astype(I32)
                return acc + jnp.sum(m.reshape(T // 8, 8, T), axis=0)
            acc = lax.fori_loop(0, i + 1, body, jnp.zeros((8, T), I32))
            return jnp.sum(acc, axis=0, keepdims=True)

        def bit_body(it, ans):
            cand = ans + lax.shift_left(jnp.int32(1), 31 - it)
            return jnp.where(count_ge(cand) >= topk, cand, ans)

        ans = lax.fori_loop(0, 32, bit_body, jnp.full((1, T), INT_MIN, I32))
        thr = jnp.maximum(ans, KEY_NEG_INF + 1)
        thr_s[...] = thr
        need_s[...] = (topk - count_ge(thr + 1)).astype(F32)

    thr = _key_to_float(thr_s[...])
    need = need_s[...]
    tri = tri_ref[...]

    def bias_body(kt, run):
        key = sc_s[kt]
        tie = key == thr
        upto = _dot(tri, jnp.where(tie, 1.0, 0.0).astype(BF16)) + run
        sel = (key > thr) | (tie & (upto <= need))
        bias_s[kt] = jnp.where(sel, 0.0, MASK_NEG)
        return upto[T - 1:T, :]

    lax.fori_loop(0, i + 1, bias_body, jnp.zeros((1, T), F32))

    qr = _rope(q_ref[...].astype(F32), cos, sin_a, sin_b).astype(BF16)
    qcat = (_dot(qr, wq_ref[...]) * LOG2E).astype(BF16)
    for h in range(nh):
        qs_s[h] = qcat[:, h * 2 * LANES:(h + 1) * 2 * LANES]
    m_s[...] = jnp.full((nh, 1, T), MASK_NEG, F32)
    acc_s[...] = jnp.zeros((nh, DSA_KV_RANK + ONES_ROWS, T), F32)

    def att_body(kt, carry):
        r0 = pl.multiple_of(kt * T, T)
        kv = kv_s[pl.ds(r0, T), :]
        vt = vt_s[kt]
        bias = bias_s[kt]
        lg_all = _dot_nt(kv, qs_s[...].reshape(nh * T, 2 * LANES))
        for h in range(nh):
            lg = lg_all[:, h * T:(h + 1) * T] + bias
            m_old = m_s[h]
            m_new = jnp.maximum(m_old, jnp.max(lg, axis=0, keepdims=True))
            alpha = jnp.exp2(m_old - m_new)
            p = jnp.exp2(lg - m_new)
            acc_s[h] = alpha * acc_s[h] + _dot(vt, p.astype(BF16))
            m_s[h] = m_new
        return carry

    lax.fori_loop(0, i + 1, att_body, 0)

    outs = []
    for h in range(nh):
        acc = acc_s[h]
        outs.append((acc[0:DSA_KV_RANK, :] / acc[DSA_KV_RANK:DSA_KV_RANK + 1, :]).astype(BF16))
    y_t = _dot(wuv_ref[...], jnp.concatenate(outs, axis=0))
    z = z_ref[...].astype(F32)
    out_ref[...] = (y_t.T * (z * jax.nn.sigmoid(z))).astype(BF16)


def _dsa(pf, pb, cos, sin_a, sin_b, kv_norm_w, wq, wuv, tri):
    b, s, _ = pf.shape
    T = tri.shape[0]
    topk = min(INDEX_TOPK, s // 4)
    nkt = s // T
    kern = functools.partial(_dsa_kernel, T=T, topk=topk, seq=s)

    def pspec(width, col):
        return pl.BlockSpec((None, T, width), lambda bb, i: (bb, i, col // width))

    def tspec():
        return pl.BlockSpec((T, LANES), lambda bb, i: (i, 0))

    return pl.pallas_call(
        kern,
        grid=(b, nkt),
        in_specs=[
            pspec(512, C_DSA_Q), pspec(LANES, F_CKV), pspec(LANES, F_SMALL), pspec(256, F_IDX_Q),
            pspec(512, C_DSA_Z), tspec(), tspec(), tspec(),
            pl.BlockSpec((1, DSA_KV_RANK), lambda bb, i: (0, 0)),
            pl.BlockSpec(wq.shape, lambda bb, i: (0, 0)),
            pl.BlockSpec(wuv.shape, lambda bb, i: (0, 0)),
            pl.BlockSpec((T, T), lambda bb, i: (0, 0)),
        ],
        out_specs=pl.BlockSpec((None, T, BRANCH_WIDTH), lambda bb, i: (bb, i, 0)),
        out_shape=jax.ShapeDtypeStruct((b, s, BRANCH_WIDTH), BF16),
        scratch_shapes=[
            pltpu.VMEM((s, 2 * LANES), BF16),
            pltpu.VMEM((s, 2 * LANES), BF16),
            pltpu.VMEM((nkt, T, T), F32),
            pltpu.VMEM((nkt, T, T), F32),
            pltpu.VMEM((DSA_HEADS, T, 2 * LANES), BF16),
            pltpu.VMEM((1, T), I32),
            pltpu.VMEM((1, T), F32),
            pltpu.VMEM((DSA_HEADS, 1, T), F32),
            pltpu.VMEM((DSA_HEADS, DSA_KV_RANK + ONES_ROWS, T), F32),
            pltpu.VMEM((nkt, DSA_KV_RANK + ONES_ROWS, T), BF16),
        ],
        compiler_params=pltpu.CompilerParams(
            dimension_semantics=("arbitrary", "arbitrary"), vmem_limit_bytes=VMEM_LIMIT),
        name="dsa",
    )(pb, pf, pf, pf, pb, cos, sin_a, sin_b, kv_norm_w.reshape(1, DSA_KV_RANK), wq, wuv, tri)


def _sb_kernel(q_ref, k_ref, v_ref, z_ref, u_ref, out_ref, *, T):
    i = pl.program_id(2)
    nhb = LANES // SB_HEAD_DIM
    lane = lax.broadcasted_iota(I32, (T, LANES), 1)
    in_head = [(lane >= hh * SB_HEAD_DIM) & (lane < (hh + 1) * SB_HEAD_DIM) for hh in range(nhb)]
    q = q_ref[...]
    u = u_ref[...]
    qs = jnp.concatenate([jnp.where(in_head[hh], q, jnp.zeros_like(q)) for hh in range(nhb)], axis=0)
    rr = lax.broadcasted_iota(I32, (nhb * T, T), 0) & (T - 1)
    cc = lax.broadcasted_iota(I32, (nhb * T, T), 1)
    strict = cc < rr

    def tile(kt, carry, acc, diag):
        r0 = pl.multiple_of(kt * T, T)
        k = k_ref[pl.ds(r0, T), :]
        v = v_ref[pl.ds(r0, T), :]
        zz = _dot_nt(qs, k)
        lsz = _log_sigmoid(zz)
        ls = lsz - zz
        if diag:
            ls = jnp.where(strict, ls, 0.0)
        la = _dot(ls.astype(BF16), u)
        a = jnp.exp(lsz + la + carry)
        if diag:
            a = jnp.where(strict, a, 0.0)
        acc = acc + _dot(a.astype(BF16), v)
        carry = carry + jnp.sum(ls, axis=1, keepdims=True)
        return carry, acc

    carry, acc = tile(i, jnp.zeros((nhb * T, 1), F32), jnp.zeros((nhb * T, LANES), F32), True)

    def cond(st):
        j, _, _, cmax = st
        return (j < i) & (cmax > F32_EXP_UNDERFLOW)

    def body(st):
        j, carry, acc, _ = st
        carry, acc = tile(i - 1 - j, carry, acc, False)
        return j + 1, carry, acc, jnp.max(carry)

    _, _, acc, _ = lax.while_loop(cond, body, (jnp.int32(0), carry, acc, jnp.max(carry)))
    result = acc[0:T, :]
    for hh in range(1, nhb):
        result = jnp.where(in_head[hh], acc[hh * T:(hh + 1) * T, :], result)
    z = z_ref[...].astype(F32)
    out_ref[...] = (result * (z * jax.nn.sigmoid(z))).astype(BF16)


def _sb(pb, u):
    b, s, _ = pb.shape
    T = u.shape[0]
    nblk = SB_HEADS * SB_HEAD_DIM // LANES

    def qspec(col):
        return pl.BlockSpec((None, T, LANES), lambda bb, hp, i: (bb, i, col // LANES + hp))

    def kspec(col):
        return pl.BlockSpec((None, s, LANES), lambda bb, hp, i: (bb, 0, col // LANES + hp))

    return pl.pallas_call(
        functools.partial(_sb_kernel, T=T),
        grid=(b, nblk, s // T),
        in_specs=[qspec(C_SB_Q), kspec(C_SB_K), kspec(C_SB_V), qspec(C_SB_Z),
                  pl.BlockSpec((T, T), lambda bb, hp, i: (0, 0))],
        out_specs=pl.BlockSpec((None, T, LANES), lambda bb, hp, i: (bb, i, hp)),
        out_shape=jax.ShapeDtypeStruct((b, s, BRANCH_WIDTH), BF16),
        compiler_params=pltpu.CompilerParams(
            dimension_semantics=("arbitrary", "arbitrary", "arbitrary"),
            vmem_limit_bytes=VMEM_LIMIT),
        name="stick_breaking",
    )(pb, pb, pb, pb, u)


def _ml_kernel(q_ref, k_ref, v_ref, o_ref, z_ref, sm_ref, cwq_ref, cwk_ref, gb_ref, nw_ref, tri_ref,
               out_ref, xq_s, xk_s, c_s, n_s, m_s, *, L):
    c = pl.program_id(1)
    pad = 8

    @pl.when(c == 0)
    def _init():
        xq_s[0:pad, :] = jnp.zeros((pad, xq_s.shape[1]), F32)
        xk_s[0:pad, :] = jnp.zeros((pad, xk_s.shape[1]), F32)
        c_s[...] = jnp.zeros(c_s.shape, F32)
        n_s[...] = jnp.zeros(n_s.shape, F32)
        m_s[...] = jnp.zeros(m_s.shape, F32)

    def conv_silu(x_ref, xs, cw_ref):
        xs[pad:pad + L, :] = x_ref[...].astype(F32)
        w = cw_ref[...]
        y = xs[pad:pad + L, :] * w[ML_CONV - 1:ML_CONV, :]
        for j in range(1, ML_CONV):
            y = y + xs[pad - j:pad - j + L, :] * w[ML_CONV - 1 - j:ML_CONV - j, :]
        xs[0:pad, :] = xs[L:L + pad, :]
        return y * jax.nn.sigmoid(y)

    qc = conv_silu(q_ref, xq_s, cwq_ref)
    kc = conv_silu(k_ref, xk_s, cwk_ref) * (ML_HEAD_DIM ** -0.5)

    g = sm_ref[...] + gb_ref[...]
    lf = _log_sigmoid(g)
    tri = tri_ref[...]
    f_hi, f_mid, f_lo = _split3(lf)
    bcum = _dot(tri, f_hi) + (_dot(tri, f_mid) + _dot(tri, f_lo))
    g_t = g.T
    b_t = bcum.T
    rr = lax.broadcasted_iota(I32, (L, L), 0)
    cc = lax.broadcasted_iota(I32, (L, L), 1)
    tril = cc <= rr

    for h in range(ML_HEADS):
        hs = slice(h * ML_HEAD_DIM, (h + 1) * ML_HEAD_DIM)
        q = qc[:, hs]
        k = kc[:, hs]
        v = v_ref[:, hs]
        qb, kb, vb = q.astype(BF16), k.astype(BF16), v.astype(BF16)
        i_col = g[:, SM_MI + h:SM_MI + h + 1]
        b_col = bcum[:, SM_MF + h:SM_MF + h + 1]
        i_row = g_t[SM_MI + h:SM_MI + h + 1, :]
        b_row = b_t[SM_MF + h:SM_MF + h + 1, :]
        m_prev = m_s[h][:, 0:1]
        c_mem = c_s[h]
        n_row = n_s[h]

        log_d = b_col - b_row + i_row
        log_inter = b_col + m_prev
        m_row = jnp.maximum(jnp.max(jnp.where(tril, log_d, -jnp.inf), axis=1, keepdims=True),
                            log_inter)
        w_intra = jnp.where(tril, jnp.exp(log_d - m_row), 0.0)
        w_inter = jnp.exp(log_inter - m_row)
        sc = _dot_nt(qb, kb) * w_intra
        num = _dot(sc.astype(BF16), vb) + w_inter * _dot(qb, c_mem.astype(BF16))
        den = (jnp.sum(sc, axis=1, keepdims=True)
               + w_inter * jnp.sum(q * n_row, axis=1, keepdims=True))
        hval = num / jnp.maximum(jnp.abs(den), jnp.exp(-m_row))

        b_last = b_col[L - 1:L, :]
        log_w = b_last - b_col + i_col
        m_new = jnp.maximum(b_last + m_prev, jnp.max(log_w, axis=0, keepdims=True))
        w_upd = jnp.exp(log_w - m_new)
        decay = jnp.exp(b_last + m_prev - m_new)
        kw = k * w_upd
        c_s[h] = decay * c_mem + _dot(kw.T.astype(BF16), vb)
        n_s[h] = decay * n_row + jnp.sum(kw, axis=0, keepdims=True)
        m_s[h] = jnp.broadcast_to(m_new, (1, LANES))

        mu = jnp.mean(hval, axis=-1, keepdims=True)
        d = hval - mu
        var = jnp.mean(d * d, axis=-1, keepdims=True)
        y = d * lax.rsqrt(var + NORM_EPS) * nw_ref[:, hs]
        z = z_ref[:, hs].astype(F32)
        y = y * jax.nn.sigmoid(o_ref[:, hs].astype(F32)) * (z * jax.nn.sigmoid(z))
        out_ref[:, hs] = y.astype(BF16)


def _ml(pf, pb, conv_w, gate_bias, norm_w, tri):
    b, s, _ = pb.shape
    L = tri.shape[0]
    w = ML_HEADS * ML_HEAD_DIM

    def pspec(width, col):
        return pl.BlockSpec((None, L, width), lambda bb, c: (bb, c, col // width))

    def full(shape):
        return pl.BlockSpec(shape, lambda bb, c: (0,) * len(shape))

    return pl.pallas_call(
        functools.partial(_ml_kernel, L=L),
        grid=(b, s // L),
        in_specs=[pspec(w, C_ML_Q), pspec(w, C_ML_K), pspec(w, C_ML_V), pspec(w, C_ML_O),
                  pspec(w, C_ML_Z), pspec(LANES, F_SMALL),
                  full((ML_CONV, w)), full((ML_CONV, w)), full((1, LANES)), full((1, w)),
                  full((L, L))],
        out_specs=pl.BlockSpec((None, L, w), lambda bb, c: (bb, c, 0)),
        out_shape=jax.ShapeDtypeStruct((b, s, BRANCH_WIDTH), BF16),
        scratch_shapes=[
            pltpu.VMEM((L + 8, w), F32), pltpu.VMEM((L + 8, w), F32),
            pltpu.VMEM((ML_HEADS, ML_HEAD_DIM, ML_HEAD_DIM), F32),
            pltpu.VMEM((ML_HEADS, 1, ML_HEAD_DIM), F32),
            pltpu.VMEM((ML_HEADS, 1, LANES), F32),
        ],
        compiler_params=pltpu.CompilerParams(
            dimension_semantics=("arbitrary", "arbitrary"), vmem_limit_bytes=VMEM_LIMIT),
        name="mlstm",
    )(pb, pb, pb, pb, pb, pf, conv_w[:, :w], conv_w[:, w:], gate_bias, norm_w.reshape(1, w), tri)


def _out_kernel(ya_ref, yb_ref, yc_ref, mg_ref, x_ref, wb_ref, wo_ref, fw_ref, o_ref, *, final):
    mixed = None
    for gi, y_ref in enumerate((ya_ref, yb_ref, yc_ref)):
        proj = _dot(y_ref[...], wb_ref[gi])
        gate = jax.nn.sigmoid(mg_ref[:, gi * D_MODEL:(gi + 1) * D_MODEL].astype(F32))
        mixed = gate * proj if mixed is None else mixed + gate * proj
    out = x_ref[...] + _dot(mixed.astype(BF16), wo_ref[...])
    if final:
        ms = jnp.mean(out * out, axis=-1, keepdims=True)
        out = out * lax.rsqrt(ms + NORM_EPS) * fw_ref[...]
    o_ref[...] = out


def _out_proj(ya, yb, yc, p2d, x2d, wb, wo, layer, final_w, final):
    m = x2d.shape[0]
    tm = min(512, m)
    mw = N_BRANCHES * D_MODEL

    def rows(width, col=0):
        return pl.BlockSpec((tm, width), lambda i: (i, col // width))

    return pl.pallas_call(
        functools.partial(_out_kernel, final=final),
        grid=(m // tm,),
        in_specs=[rows(BRANCH_WIDTH), rows(BRANCH_WIDTH), rows(BRANCH_WIDTH), rows(mw, C_MERGE),
                  rows(D_MODEL),
                  pl.BlockSpec((None,) + wb.shape[1:], lambda i: (layer, 0, 0, 0)),
                  pl.BlockSpec((None,) + wo.shape[1:], lambda i: (layer, 0, 0)),
                  pl.BlockSpec((1, D_MODEL), lambda i: (0, 0))],
        out_specs=rows(D_MODEL),
        out_shape=jax.ShapeDtypeStruct((m, D_MODEL), F32),
        compiler_params=pltpu.CompilerParams(
            dimension_semantics=("arbitrary",), vmem_limit_bytes=VMEM_LIMIT),
        name="out_proj",
    )(ya, yb, yc, p2d, x2d, wb, wo, final_w.reshape(1, D_MODEL))


def _arrange_w_in(w_in):
    def c(a, b_):
        return w_in[..., a:b_]
    small_pad = jnp.zeros(w_in.shape[:-1] + (LANES - 92,), w_in.dtype)
    cols_f = [
        c(_O_CKV, _O_KROPE),
        c(_O_IDX_K, _O_IDX_W), c(_O_KROPE, _O_IDX_Q), c(_O_IDX_W, _O_DSA_Z),
        c(_O_ML_I, _O_ML_F), c(_O_ML_F, _O_ML_O), small_pad,
        c(_O_IDX_Q, _O_IDX_K),
    ]
    cols_b = [
        c(_O_MERGE, _O_END), c(_O_DSA_Q, _O_CKV) * DSA_HEAD_DIM ** -0.5, c(_O_DSA_Z, _O_SB_Q),
        c(_O_SB_Q, _O_SB_K) * SB_HEAD_DIM ** -0.5, c(_O_SB_K, _O_ML_QK),
        c(_O_ML_QK, _O_ML_V), c(_O_ML_V, _O_ML_I), c(_O_ML_O, _O_MERGE),
    ]
    return jnp.concatenate(cols_f, axis=-1), jnp.concatenate(cols_b, axis=-1)


def _rope_tables(seq):
    pos = jnp.arange(seq, dtype=F32)
    inv = ROPE_THETA ** (-jnp.arange(0, ROPE_DIM, 2, dtype=F32) / ROPE_DIM)
    ang = pos[:, None] * inv[None, :]
    cos, sin = jnp.cos(ang), jnp.sin(ang)
    half = ROPE_DIM // 2
    r = jnp.arange(LANES) % DSA_HEAD_DIM
    cos_l = cos[:, r % half]
    sin_l = sin[:, r % half]
    cos_t = jnp.where(r < ROPE_DIM, cos_l, 1.0)
    sin_a = jnp.where(r < half, -sin_l, 0.0)
    sin_b = jnp.where((r >= half) & (r < ROPE_DIM), sin_l, 0.0)
    return cos_t, sin_a, sin_b


def _dsa_query_matrix(w_uk):
    rank, nh, nope = w_uk.shape
    wq = jnp.zeros((nh, DSA_HEAD_DIM, nh, 2 * LANES), F32)
    eye = jnp.eye(ROPE_DIM, dtype=F32)
    for h in range(nh):
        wq = wq.at[h, ROPE_DIM:, h, :rank].set(w_uk[:, h, :].T)
        wq = wq.at[h, :ROPE_DIM, h, LANES + SM_KR:LANES + SM_KR + ROPE_DIM].set(eye)
    return wq.reshape(nh * DSA_HEAD_DIM, nh * 2 * LANES).astype(BF16)


def _dsa_value_matrix(w_uv):
    rank, nh, vd = w_uv.shape
    wv = jnp.zeros((nh, rank, nh, vd), F32)
    for h in range(nh):
        wv = wv.at[h, :, h, :].set(w_uv[:, h, :])
    return wv.reshape(nh * rank, nh * vd).T.astype(BF16)


def kernel(x, w_in, w_dsa_uk, w_dsa_uv, dsa_kv_norm_w, ml_conv_w, ml_i_bias, ml_f_bias, ml_norm_w,
           w_branch, w_out, norm_w, final_norm_w):
    bsz, seq, _ = x.shape
    depth = w_in.shape[0]
    cos_t, sin_a, sin_b = _rope_tables(seq)
    w_f, w_b = _arrange_w_in(w_in.astype(BF16))
    wb = w_branch.astype(BF16)
    wo = w_out.astype(BF16)
    T = min(256, seq)
    idx = jnp.arange(T)
    u_after = (idx[:, None] > idx[None, :]).astype(BF16)
    tri_incl = (idx[None, :] <= idx[:, None]).astype(BF16)
    gate_bias = jnp.zeros((depth, 1, LANES), F32)
    gate_bias = gate_bias.at[:, 0, SM_MI:SM_MI + ML_HEADS].set(ml_i_bias)
    gate_bias = gate_bias.at[:, 0, SM_MF:SM_MF + ML_HEADS].set(ml_f_bias)

    x2d = x.reshape(bsz * seq, D_MODEL)
    for l in range(depth):
        pf2d = _in_proj(x2d, norm_w[l], w_f, l, PF_COLS, F32, "in_proj_f32")
        pb2d = _in_proj(x2d, norm_w[l], w_b, l, PB_COLS // 4, BF16, "in_proj_bf16")
        pf = pf2d.reshape(bsz, seq, PF_COLS)
        pb = pb2d.reshape(bsz, seq, PB_COLS)
        ya = _dsa(pf, pb, cos_t, sin_a, sin_b, dsa_kv_norm_w[l],
                  _dsa_query_matrix(w_dsa_uk[l]), _dsa_value_matrix(w_dsa_uv[l]), tri_incl)
        yb = _sb(pb, u_after)
        yc = _ml(pf, pb, ml_conv_w[l], gate_bias[l], ml_norm_w[l], tri_incl)
        x2d = _out_proj(ya.reshape(-1, BRANCH_WIDTH), yb.reshape(-1, BRANCH_WIDTH),
                        yc.reshape(-1, BRANCH_WIDTH), pb2d, x2d, wb, wo, l, final_norm_w,
                        final=(l == depth - 1))
    return x2d.reshape(bsz, seq, D_MODEL)
```

```python
import functools

import jax
import jax.numpy as jnp
from jax import lax
from jax.experimental import pallas as pl
from jax.experimental.pallas import tpu as pltpu

F32 = jnp.float32
BF16 = jnp.bfloat16
I32 = jnp.int32

D_MODEL = 1024
ROPE_THETA = 500000.0
ROPE_DIM = 16
NORM_EPS = 1e-6
DSA_HEADS = 8
DSA_HEAD_DIM = 64
DSA_NOPE_DIM = DSA_HEAD_DIM - ROPE_DIM
DSA_KV_RANK = 128
DSA_V_DIM = 64
IDX_HEADS = 4
IDX_DIM = 64
INDEX_TOPK = 256
SB_HEADS = 8
SB_HEAD_DIM = 64
ML_HEADS = 4
ML_HEAD_DIM = 128
ML_CONV = 4
N_BRANCHES = 3
BRANCH_WIDTH = 512

LANES = 128
VMEM_LIMIT = 56 * 1024 * 1024

_O_DSA_Q, _O_CKV, _O_KROPE, _O_IDX_Q, _O_IDX_K, _O_IDX_W = 0, 512, 640, 656, 912, 976
_O_DSA_Z, _O_SB_Q, _O_SB_K, _O_SB_V, _O_SB_Z = 980, 1492, 2004, 2516, 3028
_O_ML_QK, _O_ML_V, _O_ML_I, _O_ML_F, _O_ML_O, _O_ML_Z, _O_MERGE, _O_END = (
    3540, 4564, 5076, 5080, 5084, 5596, 6108, 9180)

F_CKV, F_SMALL, F_IDX_Q, PF_COLS = 0, 128, 256, 512
C_MERGE, C_DSA_Q, C_DSA_Z = 0, 3072, 3584
C_SB_Q, C_SB_K, C_SB_V, C_SB_Z = 4096, 4608, 5120, 5632
C_ML_Q, C_ML_K, C_ML_V, C_ML_O, C_ML_Z, PB_COLS = 6144, 6656, 7168, 7680, 8192, 8704
SM_IK, SM_KR, SM_IW, SM_MI, SM_MF = 0, 64, 80, 84, 88

KEY_NEG_INF = -2139095041
INT_MIN = -2147483648
MASK_NEG = -1e30
LOG2E = 1.4426950408889634
F32_EXP_UNDERFLOW = -104.0
ONES_ROWS = 16


def _dot(a, b):
    return jnp.dot(a, b, preferred_element_type=F32)


def _dot_nt(a, b):
    return lax.dot_general(a, b, (((1,), (1,)), ((), ())), preferred_element_type=F32)


def _split2(x):
    hi = x.astype(BF16)
    lo = (x - hi.astype(F32)).astype(BF16)
    return hi, lo


def _split3(x):
    hi = x.astype(BF16)
    r = x - hi.astype(F32)
    mid = r.astype(BF16)
    lo = (r - mid.astype(F32)).astype(BF16)
    return hi, mid, lo


def _log_sigmoid(x):
    return jnp.minimum(x, 0.0) - jnp.log(1.0 + jnp.exp(-jnp.abs(x)))


def _rope(x, cos, sin_a, sin_b):
    w = x.shape[1]
    reps = w // LANES
    if reps > 1:
        cos = jnp.concatenate([cos] * reps, axis=1)
        sin_a = jnp.concatenate([sin_a] * reps, axis=1)
        sin_b = jnp.concatenate([sin_b] * reps, axis=1)
    half = ROPE_DIM // 2
    return x * cos + pltpu.roll(x, w - half, 1) * sin_a + pltpu.roll(x, half, 1) * sin_b


def _in_proj_kernel(x_ref, nw_ref, w_ref, o_ref, h_ref):
    @pl.when(pl.program_id(1) == 0)
    def _():
        x = x_ref[...]
        ms = jnp.mean(x * x, axis=-1, keepdims=True)
        h_ref[...] = (x * lax.rsqrt(ms + NORM_EPS) * nw_ref[...]).astype(BF16)

    o_ref[...] = _dot(h_ref[...], w_ref[...]).astype(o_ref.dtype)


def _in_proj(x2d, norm_w, w_all, layer, tn, out_dtype, name):
    m = x2d.shape[0]
    n = w_all.shape[-1]
    tm = min(1024, m)
    return pl.pallas_call(
        _in_proj_kernel,
        grid=(m // tm, n // tn),
        in_specs=[
            pl.BlockSpec((tm, D_MODEL), lambda i, j: (i, 0)),
            pl.BlockSpec((1, D_MODEL), lambda i, j: (0, 0)),
            pl.BlockSpec((None, D_MODEL, tn), lambda i, j: (layer, 0, j)),
        ],
        out_specs=pl.BlockSpec((tm, tn), lambda i, j: (i, j)),
        out_shape=jax.ShapeDtypeStruct((m, n), out_dtype),
        scratch_shapes=[pltpu.VMEM((tm, D_MODEL), BF16)],
        compiler_params=pltpu.CompilerParams(
            dimension_semantics=("arbitrary", "arbitrary"), vmem_limit_bytes=VMEM_LIMIT),
        name=name,
    )(x2d, norm_w.reshape(1, D_MODEL), w_all)


def _key_to_float(key):
    return lax.bitcast_convert_type(key ^ ((key >> 31) & 0x7FFFFFFF), F32)


def _dsa_kernel(q_ref, ckv_ref, sm_ref, iq_ref, z_ref, cos_ref, sa_ref, sb_ref, kvw_ref, wq_ref,
                wuv_ref, tri_ref, out_ref, kv_s, ik_s, sc_s, qs_s, thr_s, need_s,
                m_s, acc_s, vt_s, *, T, topk, seq):
    i = pl.program_id(1)
    nh = DSA_HEADS
    cos, sin_a, sin_b = cos_ref[...], sa_ref[...], sb_ref[...]
    lane = lax.broadcasted_iota(I32, (T, LANES), 1)

    sm = _rope(sm_ref[...], cos, sin_a, sin_b)
    ckv = ckv_ref[...]
    ckv_n = ckv * lax.rsqrt(jnp.mean(ckv * ckv, axis=-1, keepdims=True) + NORM_EPS) * kvw_ref[...]
    row0 = pl.multiple_of(i * T, T)
    kv_s[pl.ds(row0, T), 0:LANES] = ckv_n.astype(BF16)
    kr = jnp.where((lane >= SM_KR) & (lane < SM_KR + ROPE_DIM), sm, 0.0)
    kv_s[pl.ds(row0, T), LANES:2 * LANES] = kr.astype(BF16)
    ik = jnp.where(lane < IDX_DIM, sm, 0.0)
    ik_hi = ik.astype(BF16).astype(F32)
    ik_s[pl.ds(row0, T), 0:LANES] = (ik_hi + pltpu.roll(ik - ik_hi, IDX_DIM, 1)).astype(BF16)
    ik_s[pl.ds(row0, T), LANES:2 * LANES] = ik_hi.astype(BF16)
    vt_s[i, 0:DSA_KV_RANK, :] = ckv_n.T.astype(BF16)
    vt_s[i, DSA_KV_RANK:DSA_KV_RANK + ONES_ROWS, :] = jnp.ones((ONES_ROWS, T), BF16)

    iq = _rope(iq_ref[...], cos, sin_a, sin_b)
    iq_parts = []
    for h in range(IDX_HEADS):
        t = iq[:, (h // 2) * LANES:(h // 2 + 1) * LANES]
        if h % 2:
            t = pltpu.roll(t, IDX_DIM, 1)
        t = jnp.where(lane < IDX_DIM, t, 0.0)
        t_hi = t.astype(BF16).astype(F32)
        iq_parts.append(jnp.concatenate([t_hi + pltpu.roll(t_hi, IDX_DIM, 1), t - t_hi],
                                        axis=1).astype(BF16))
    iq_all = jnp.concatenate(iq_parts, axis=0)
    sm_t = sm.T
    w_rows = [sm_t[SM_IW + h:SM_IW + h + 1, :] for h in range(IDX_HEADS)]

    def score_tile(kt):
        r0 = pl.multiple_of(kt * T, T)
        kk = ik_s[pl.ds(r0, T), :]
        d_all = _dot_nt(kk, iq_all)
        s = jnp.zeros((T, T), F32)
        for h in range(IDX_HEADS):
            s = s + w_rows[h] * jnp.maximum(d_all[:, h * T:(h + 1) * T], 0.0)
        return jnp.where(s == 0.0, 0.0, s)

    def score_body(kt, carry):
        sc_s[kt] = score_tile(kt)
        return carry

    lax.fori_loop(0, i, score_body, 0)
    kpos = lax.broadcasted_iota(I32, (T, T), 0)
    qpos = lax.broadcasted_iota(I32, (T, T), 1)
    sc_s[i] = jnp.where(kpos <= qpos, score_tile(i), -jnp.inf)

    thr_s[...] = jnp.full((1, T), KEY_NEG_INF + 1, I32)
    need_s[...] = jnp.full((1, T), seq, F32)

    @pl.when((i + 1) * T > topk)
    def _select():
        def count_ge(cand_key):
            cand = _key_to_float(cand_key)

            def body(kt, acc):
                m = jnp.where(sc_s[kt] >= cand, 1, 0).astype(I32)
                return acc + jnp.sum(m.reshape(T // 8, 8, T), axis=0)
            acc = lax.fori_loop(0, i + 1, body, jnp.zeros((8, T), I32))
            return jnp.sum(acc, axis=0, keepdims=True)

        def bit_body(it, ans):
            cand = ans + lax.shift_left(jnp.int32(1), 31 - it)
            return jnp.where(count_ge(cand) >= topk, cand, ans)

        ans = lax.fori_loop(0, 32, bit_body, jnp.full((1, T), INT_MIN, I32))
        thr = jnp.maximum(ans, KEY_NEG_INF + 1)
        thr_s[...] = thr
        need_s[...] = (topk - count_ge(thr + 1)).astype(F32)

    thr = _key_to_float(thr_s[...])
    need = need_s[...]
    tri = tri_ref[...]

    qr = _rope(q_ref[...].astype(F32), cos, sin_a, sin_b).astype(BF16)
    qcat = (_dot(qr, wq_ref[...]) * LOG2E).astype(BF16)
    for h in range(nh):
        qs_s[h] = qcat[:, h * 2 * LANES:(h + 1) * 2 * LANES]
    m_s[...] = jnp.full((nh, 1, T), MASK_NEG, F32)
    acc_s[...] = jnp.zeros((nh, DSA_KV_RANK + ONES_ROWS, T), F32)

    def att_body(kt, run):
        r0 = pl.multiple_of(kt * T, T)
        kv = kv_s[pl.ds(r0, T), :]
        vt = vt_s[kt]
        key = sc_s[kt]
        tie = key == thr
        upto = _dot(tri, jnp.where(tie, 1.0, 0.0).astype(BF16)) + run
        bias = jnp.where((key > thr) | (tie & (upto <= need)), 0.0, MASK_NEG)
        lg_all = _dot_nt(kv, qs_s[...].reshape(nh * T, 2 * LANES))
        for h in range(nh):
            lg = lg_all[:, h * T:(h + 1) * T] + bias
            m_old = m_s[h]
            m_new = jnp.maximum(m_old, jnp.max(lg, axis=0, keepdims=True))
            alpha = jnp.exp2(m_old - m_new)
            p = jnp.exp2(lg - m_new)
            acc_s[h] = alpha * acc_s[h] + _dot(vt, p.astype(BF16))
            m_s[h] = m_new
        return upto[T - 1:T, :]

    lax.fori_loop(0, i + 1, att_body, jnp.zeros((1, T), F32))

    outs = []
    for h in range(nh):
        acc = acc_s[h]
        outs.append((acc[0:DSA_KV_RANK, :] / acc[DSA_KV_RANK:DSA_KV_RANK + 1, :]).astype(BF16))
    y_t = _dot(wuv_ref[...], jnp.concatenate(outs, axis=0))
    z = z_ref[...].astype(F32)
    out_ref[...] = (y_t.T * (z * jax.nn.sigmoid(z))).astype(BF16)


def _dsa(pf, pb, cos, sin_a, sin_b, kv_norm_w, wq, wuv, tri):
    b, s, _ = pf.shape
    T = tri.shape[0]
    topk = min(INDEX_TOPK, s // 4)
    nkt = s // T
    kern = functools.partial(_dsa_kernel, T=T, topk=topk, seq=s)

    def pspec(width, col):
        return pl.BlockSpec((None, T, width), lambda bb, i: (bb, i, col // width))

    def tspec():
        return pl.BlockSpec((T, LANES), lambda bb, i: (i, 0))

    return pl.pallas_call(
        kern,
        grid=(b, nkt),
        in_specs=[
            pspec(512, C_DSA_Q), pspec(LANES, F_CKV), pspec(LANES, F_SMALL), pspec(256, F_IDX_Q),
            pspec(512, C_DSA_Z), tspec(), tspec(), tspec(),
            pl.BlockSpec((1, DSA_KV_RANK), lambda bb, i: (0, 0)),
            pl.BlockSpec(wq.shape, lambda bb, i: (0, 0)),
            pl.BlockSpec(wuv.shape, lambda bb, i: (0, 0)),
            pl.BlockSpec((T, T), lambda bb, i: (0, 0)),
        ],
        out_specs=pl.BlockSpec((None, T, BRANCH_WIDTH), lambda bb, i: (bb, i, 0)),
        out_shape=jax.ShapeDtypeStruct((b, s, BRANCH_WIDTH), BF16),
        scratch_shapes=[
            pltpu.VMEM((s, 2 * LANES), BF16),
            pltpu.VMEM((s, 2 * LANES), BF16),
            pltpu.VMEM((nkt, T, T), F32),
            pltpu.VMEM((DSA_HEADS, T, 2 * LANES), BF16),
            pltpu.VMEM((1, T), I32),
            pltpu.VMEM((1, T), F32),
            pltpu.VMEM((DSA_HEADS, 1, T), F32),
            pltpu.VMEM((DSA_HEADS, DSA_KV_RANK + ONES_ROWS, T), F32),
            pltpu.VMEM((nkt, DSA_KV_RANK + ONES_ROWS, T), BF16),
        ],
        compiler_params=pltpu.CompilerParams(
            dimension_semantics=("arbitrary", "arbitrary"), vmem_limit_bytes=VMEM_LIMIT),
        name="dsa",
    )(pb, pf, pf, pf, pb, cos, sin_a, sin_b, kv_norm_w.reshape(1, DSA_KV_RANK), wq, wuv, tri)


def _sb_kernel(q_ref, k_ref, v_ref, z_ref, u_ref, out_ref, *, T):
    i = pl.program_id(2)
    nhb = LANES // SB_HEAD_DIM
    lane = lax.broadcasted_iota(I32, (T, LANES), 1)
    in_head = [(lane >= hh * SB_HEAD_DIM) & (lane < (hh + 1) * SB_HEAD_DIM) for hh in range(nhb)]
    q = q_ref[...]
    u = u_ref[...]
    qs = jnp.concatenate([jnp.where(in_head[hh], q, jnp.zeros_like(q)) for hh in range(nhb)], axis=0)
    rr = lax.broadcasted_iota(I32, (nhb * T, T), 0) & (T - 1)
    cc = lax.broadcasted_iota(I32, (nhb * T, T), 1)
    strict = cc < rr

    def tile(kt, carry, acc, diag):
        r0 = pl.multiple_of(kt * T, T)
        k = k_ref[pl.ds(r0, T), :]
        v = v_ref[pl.ds(r0, T), :]
        zz = _dot_nt(qs, k)
        lsz = _log_sigmoid(zz)
        ls = lsz - zz
        if diag:
            ls = jnp.where(strict, ls, 0.0)
        la = _dot(ls.astype(BF16), u)
        a = jnp.exp(lsz + la + carry)
        if diag:
            a = jnp.where(strict, a, 0.0)
        acc = acc + _dot(a.astype(BF16), v)
        carry = carry + jnp.sum(ls, axis=1, keepdims=True)
        return carry, acc

    carry, acc = tile(i, jnp.zeros((nhb * T, 1), F32), jnp.zeros((nhb * T, LANES), F32), True)

    def cond(st):
        j, _, _, cmax = st
        return (j < i) & (cmax > F32_EXP_UNDERFLOW)

    def body(st):
        j, carry, acc, _ = st
        carry, acc = tile(i - 1 - j, carry, acc, False)
        return j + 1, carry, acc, jnp.max(carry)

    _, _, acc, _ = lax.while_loop(cond, body, (jnp.int32(0), carry, acc, jnp.max(carry)))
    result = acc[0:T, :]
    for hh in range(1, nhb):
        result = jnp.where(in_head[hh], acc[hh * T:(hh + 1) * T, :], result)
    z = z_ref[...].astype(F32)
    out_ref[...] = (result * (z * jax.nn.sigmoid(z))).astype(BF16)


def _sb(pb, u):
    b, s, _ = pb.shape
    T = u.shape[0]
    nblk = SB_HEADS * SB_HEAD_DIM // LANES

    def qspec(col):
        return pl.BlockSpec((None, T, LANES), lambda bb, hp, i: (bb, i, col // LANES + hp))

    def kspec(col):
        return pl.BlockSpec((None, s, LANES), lambda bb, hp, i: (bb, 0, col // LANES + hp))

    return pl.pallas_call(
        functools.partial(_sb_kernel, T=T),
        grid=(b, nblk, s // T),
        in_specs=[qspec(C_SB_Q), kspec(C_SB_K), kspec(C_SB_V), qspec(C_SB_Z),
                  pl.BlockSpec((T, T), lambda bb, hp, i: (0, 0))],
        out_specs=pl.BlockSpec((None, T, LANES), lambda bb, hp, i: (bb, i, hp)),
        out_shape=jax.ShapeDtypeStruct((b, s, BRANCH_WIDTH), BF16),
        compiler_params=pltpu.CompilerParams(
            dimension_semantics=("arbitrary", "arbitrary", "arbitrary"),
            vmem_limit_bytes=VMEM_LIMIT),
        name="stick_breaking",
    )(pb, pb, pb, pb, u)


def _ml_kernel(q_ref, k_ref, v_ref, o_ref, z_ref, sm_ref, cwq_ref, cwk_ref, gb_ref, nw_ref, tri_ref,
               out_ref, xq_s, xk_s, c_s, n_s, m_s, *, L):
    c = pl.program_id(1)
    pad = 8

    @pl.when(c == 0)
    def _init():
        xq_s[0:pad, :] = jnp.zeros((pad, xq_s.shape[1]), F32)
        xk_s[0:pad, :] = jnp.zeros((pad, xk_s.shape[1]), F32)
        c_s[...] = jnp.zeros(c_s.shape, F32)
        n_s[...] = jnp.zeros(n_s.shape, F32)
        m_s[...] = jnp.zeros(m_s.shape, F32)

    def conv_silu(x_ref, xs, cw_ref):
        xs[pad:pad + L, :] = x_ref[...].astype(F32)
        w = cw_ref[...]
        y = xs[pad:pad + L, :] * w[ML_CONV - 1:ML_CONV, :]
        for j in range(1, ML_CONV):
            y = y + xs[pad - j:pad - j + L, :] * w[ML_CONV - 1 - j:ML_CONV - j, :]
        xs[0:pad, :] = xs[L:L + pad, :]
        return y * jax.nn.sigmoid(y)

    qc = conv_silu(q_ref, xq_s, cwq_ref)
    kc = conv_silu(k_ref, xk_s, cwk_ref) * (ML_HEAD_DIM ** -0.5)

    g = sm_ref[...] + gb_ref[...]
    lf = _log_sigmoid(g)
    tri = tri_ref[...]
    f_hi, f_mid, f_lo = _split3(lf)
    bcum = _dot(tri, f_hi) + (_dot(tri, f_mid) + _dot(tri, f_lo))
    g_t = g.T
    b_t = bcum.T
    rr = lax.broadcasted_iota(I32, (L, L), 0)
    cc = lax.broadcasted_iota(I32, (L, L), 1)
    tril = cc <= rr

    for h in range(ML_HEADS):
        hs = slice(h * ML_HEAD_DIM, (h + 1) * ML_HEAD_DIM)
        q = qc[:, hs]
        k = kc[:, hs]
        v = v_ref[:, hs]
        qb, kb, vb = q.astype(BF16), k.astype(BF16), v.astype(BF16)
        i_col = g[:, SM_MI + h:SM_MI + h + 1]
        b_col = bcum[:, SM_MF + h:SM_MF + h + 1]
        i_row = g_t[SM_MI + h:SM_MI + h + 1, :]
        b_row = b_t[SM_MF + h:SM_MF + h + 1, :]
        m_prev = m_s[h][:, 0:1]
        c_mem = c_s[h]
        n_row = n_s[h]

        log_d = b_col - b_row + i_row
        log_inter = b_col + m_prev
        m_row = jnp.maximum(jnp.max(jnp.where(tril, log_d, -jnp.inf), axis=1, keepdims=True),
                            log_inter)
        w_intra = jnp.where(tril, jnp.exp(log_d - m_row), 0.0)
        w_inter = jnp.exp(log_inter - m_row)
        sc = _dot_nt(qb, kb) * w_intra
        num = _dot(sc.astype(BF16), vb) + w_inter * _dot(qb, c_mem.astype(BF16))
        den = (jnp.sum(sc, axis=1, keepdims=True)
               + w_inter * jnp.sum(q * n_row, axis=1, keepdims=True))
        hval = num / jnp.maximum(jnp.abs(den), jnp.exp(-m_row))

        b_last = b_col[L - 1:L, :]
        log_w = b_last - b_col + i_col
        m_new = jnp.maximum(b_last + m_prev, jnp.max(log_w, axis=0, keepdims=True))
        w_upd = jnp.exp(log_w - m_new)
        decay = jnp.exp(b_last + m_prev - m_new)
        kw = k * w_upd
        c_s[h] = decay * c_mem + _dot(kw.T.astype(BF16), vb)
        n_s[h] = decay * n_row + jnp.sum(kw, axis=0, keepdims=True)
        m_s[h] = jnp.broadcast_to(m_new, (1, LANES))

        mu = jnp.mean(hval, axis=-1, keepdims=True)
        d = hval - mu
        var = jnp.mean(d * d, axis=-1, keepdims=True)
        y = d * lax.rsqrt(var + NORM_EPS) * nw_ref[:, hs]
        z = z_ref[:, hs].astype(F32)
        y = y * jax.nn.sigmoid(o_ref[:, hs].astype(F32)) * (z * jax.nn.sigmoid(z))
        out_ref[:, hs] = y.astype(BF16)


def _ml(pf, pb, conv_w, gate_bias, norm_w, tri):
    b, s, _ = pb.shape
    L = tri.shape[0]
    w = ML_HEADS * ML_HEAD_DIM

    def pspec(width, col):
        return pl.BlockSpec((None, L, width), lambda bb, c: (bb, c, col // width))

    def full(shape):
        return pl.BlockSpec(shape, lambda bb, c: (0,) * len(shape))

    return pl.pallas_call(
        functools.partial(_ml_kernel, L=L),
        grid=(b, s // L),
        in_specs=[pspec(w, C_ML_Q), pspec(w, C_ML_K), pspec(w, C_ML_V), pspec(w, C_ML_O),
                  pspec(w, C_ML_Z), pspec(LANES, F_SMALL),
                  full((ML_CONV, w)), full((ML_CONV, w)), full((1, LANES)), full((1, w)),
                  full((L, L))],
        out_specs=pl.BlockSpec((None, L, w), lambda bb, c: (bb, c, 0)),
        out_shape=jax.ShapeDtypeStruct((b, s, BRANCH_WIDTH), BF16),
        scratch_shapes=[
            pltpu.VMEM((L + 8, w), F32), pltpu.VMEM((L + 8, w), F32),
            pltpu.VMEM((ML_HEADS, ML_HEAD_DIM, ML_HEAD_DIM), F32),
            pltpu.VMEM((ML_HEADS, 1, ML_HEAD_DIM), F32),
            pltpu.VMEM((ML_HEADS, 1, LANES), F32),
        ],
        compiler_params=pltpu.CompilerParams(
            dimension_semantics=("arbitrary", "arbitrary"), vmem_limit_bytes=VMEM_LIMIT),
        name="mlstm",
    )(pb, pb, pb, pb, pb, pf, conv_w[:, :w], conv_w[:, w:], gate_bias, norm_w.reshape(1, w), tri)


def _out_kernel(ya_ref, yb_ref, yc_ref, mg_ref, x_ref, wb_ref, wo_ref, fw_ref, o_ref, *, final):
    mixed = None
    for gi, y_ref in enumerate((ya_ref, yb_ref, yc_ref)):
        proj = _dot(y_ref[...], wb_ref[gi])
        gate = jax.nn.sigmoid(mg_ref[:, gi * D_MODEL:(gi + 1) * D_MODEL].astype(F32))
        mixed = gate * proj if mixed is None else mixed + gate * proj
    out = x_ref[...] + _dot(mixed.astype(BF16), wo_ref[...])
    if final:
        ms = jnp.mean(out * out, axis=-1, keepdims=True)
        out = out * lax.rsqrt(ms + NORM_EPS) * fw_ref[...]
    o_ref[...] = out


def _out_proj(ya, yb, yc, p2d, x2d, wb, wo, layer, final_w, final):
    m = x2d.shape[0]
    tm = min(512, m)
    mw = N_BRANCHES * D_MODEL

    def rows(width, col=0):
        return pl.BlockSpec((tm, width), lambda i: (i, col // width))

    return pl.pallas_call(
        functools.partial(_out_kernel, final=final),
        grid=(m // tm,),
        in_specs=[rows(BRANCH_WIDTH), rows(BRANCH_WIDTH), rows(BRANCH_WIDTH), rows(mw, C_MERGE),
                  rows(D_MODEL),
                  pl.BlockSpec((None,) + wb.shape[1:], lambda i: (layer, 0, 0, 0)),
                  pl.BlockSpec((None,) + wo.shape[1:], lambda i: (layer, 0, 0)),
                  pl.BlockSpec((1, D_MODEL), lambda i: (0, 0))],
        out_specs=rows(D_MODEL),
        out_shape=jax.ShapeDtypeStruct((m, D_MODEL), F32),
        compiler_params=pltpu.CompilerParams(
            dimension_semantics=("arbitrary",), vmem_limit_bytes=VMEM_LIMIT),
        name="out_proj",
    )(ya, yb, yc, p2d, x2d, wb, wo, final_w.reshape(1, D_MODEL))


def _arrange_w_in(w_in):
    def c(a, b_):
        return w_in[..., a:b_]
    small_pad = jnp.zeros(w_in.shape[:-1] + (LANES - 92,), w_in.dtype)
    cols_f = [
        c(_O_CKV, _O_KROPE),
        c(_O_IDX_K, _O_IDX_W), c(_O_KROPE, _O_IDX_Q), c(_O_IDX_W, _O_DSA_Z),
        c(_O_ML_I, _O_ML_F), c(_O_ML_F, _O_ML_O), small_pad,
        c(_O_IDX_Q, _O_IDX_K),
    ]
    cols_b = [
        c(_O_MERGE, _O_END), c(_O_DSA_Q, _O_CKV) * DSA_HEAD_DIM ** -0.5, c(_O_DSA_Z, _O_SB_Q),
        c(_O_SB_Q, _O_SB_K) * SB_HEAD_DIM ** -0.5, c(_O_SB_K, _O_ML_QK),
        c(_O_ML_QK, _O_ML_V), c(_O_ML_V, _O_ML_I), c(_O_ML_O, _O_MERGE),
    ]
    return jnp.concatenate(cols_f, axis=-1), jnp.concatenate(cols_b, axis=-1)


def _rope_tables(seq):
    pos = jnp.arange(seq, dtype=F32)
    inv = ROPE_THETA ** (-jnp.arange(0, ROPE_DIM, 2, dtype=F32) / ROPE_DIM)
    ang = pos[:, None] * inv[None, :]
    cos, sin = jnp.cos(ang), jnp.sin(ang)
    half = ROPE_DIM // 2
    r = jnp.arange(LANES) % DSA_HEAD_DIM
    cos_l = cos[:, r % half]
    sin_l = sin[:, r % half]
    cos_t = jnp.where(r < ROPE_DIM, cos_l, 1.0)
    sin_a = jnp.where(r < half, -sin_l, 0.0)
    sin_b = jnp.where((r >= half) & (r < ROPE_DIM), sin_l, 0.0)
    return cos_t, sin_a, sin_b


def _dsa_query_matrix(w_uk):
    rank, nh, nope = w_uk.shape
    wq = jnp.zeros((nh, DSA_HEAD_DIM, nh, 2 * LANES), F32)
    eye = jnp.eye(ROPE_DIM, dtype=F32)
    for h in range(nh):
        wq = wq.at[h, ROPE_DIM:, h, :rank].set(w_uk[:, h, :].T)
        wq = wq.at[h, :ROPE_DIM, h, LANES + SM_KR:LANES + SM_KR + ROPE_DIM].set(eye)
    return wq.reshape(nh * DSA_HEAD_DIM, nh * 2 * LANES).astype(BF16)


def _dsa_value_matrix(w_uv):
    rank, nh, vd = w_uv.shape
    wv = jnp.zeros((nh, rank, nh, vd), F32)
    for h in range(nh):
        wv = wv.at[h, :, h, :].set(w_uv[:, h, :])
    return wv.reshape(nh * rank, nh * vd).T.astype(BF16)


def kernel(x, w_in, w_dsa_uk, w_dsa_uv, dsa_kv_norm_w, ml_conv_w, ml_i_bias, ml_f_bias, ml_norm_w,
           w_branch, w_out, norm_w, final_norm_w):
    bsz, seq, _ = x.shape
    depth = w_in.shape[0]
    cos_t, sin_a, sin_b = _rope_tables(seq)
    w_f, w_b = _arrange_w_in(w_in.astype(BF16))
    wb = w_branch.astype(BF16)
    wo = w_out.astype(BF16)
    T = min(256, seq)
    idx = jnp.arange(T)
    u_after = (idx[:, None] > idx[None, :]).astype(BF16)
    tri_incl = (idx[None, :] <= idx[:, None]).astype(BF16)
    gate_bias = jnp.zeros((depth, 1, LANES), F32)
    gate_bias = gate_bias.at[:, 0, SM_MI:SM_MI + ML_HEADS].set(ml_i_bias)
    gate_bias = gate_bias.at[:, 0, SM_MF:SM_MF + ML_HEADS].set(ml_f_bias)

    x2d = x.reshape(bsz * seq, D_MODEL)
    for l in range(depth):
        pf2d = _in_proj(x2d, norm_w[l], w_f, l, PF_COLS, F32, "in_proj_f32")
        pb2d = _in_proj(x2d, norm_w[l], w_b, l, PB_COLS // 4, BF16, "in_proj_bf16")
        pf = pf2d.reshape(bsz, seq, PF_COLS)
        pb = pb2d.reshape(bsz, seq, PB_COLS)
        ya = _dsa(pf, pb, cos_t, sin_a, sin_b, dsa_kv_norm_w[l],
                  _dsa_query_matrix(w_dsa_uk[l]), _dsa_value_matrix(w_dsa_uv[l]), tri_incl)
        yb = _sb(pb, u_after)
        yc = _ml(pf, pb, ml_conv_w[l], gate_bias[l], ml_norm_w[l], tri_incl)
        x2d = _out_proj(ya.reshape(-1, BRANCH_WIDTH), yb.reshape(-1, BRANCH_WIDTH),
                        yc.reshape(-1, BRANCH_WIDTH), pb2d, x2d, wb, wo, l, final_norm_w,
                        final=(l == depth - 1))
    return x2d.reshape(bsz, seq, D_MODEL)
```
